```python
import jax
import jax.numpy as jnp
from jax import lax
import numpy as np

D_MODEL = 1024
BATCH = 8
SEQ = 2048
DEPTH = 2
DEC_BATCH = 32
DEC_SEQ = 4
PAST_LEN = 8192
PAGE_SIZE = 128

SGU_GROUPS = 4
SGU_GROUP_DIM = D_MODEL // 8
SGU_DIM = SGU_GROUPS * SGU_GROUP_DIM
CHUNK = 128
MOBA_HEADS = 8
MOBA_HEAD_DIM = D_MODEL // 16
MOBA_DIM = MOBA_HEADS * MOBA_HEAD_DIM
MOBA_BLOCK = 256
MOBA_TOPK = 3
MOBA_Q_CHUNK = 32
MEM_LEN = 256
MEM_HEADS = 4
MEM_HEAD_DIM = D_MODEL // 8
MEM_DIM = MEM_HEADS * MEM_HEAD_DIM
D_FF = ((8 * D_MODEL // 3 + 255) // 256) * 256
ROPE_THETA = 10000.0
NORM_EPS = 1e-6
MASK_VALUE = -1e30
IN_SPLITS = [SGU_DIM, 2 * SGU_DIM, 2 * SGU_DIM + MOBA_DIM, 2 * SGU_DIM + 2 * MOBA_DIM, 2 * SGU_DIM + 3 * MOBA_DIM, 2 * SGU_DIM + 3 * MOBA_DIM + MEM_DIM]
IN_COLS = IN_SPLITS[-1] + 3 * D_MODEL

kernel_name = "hybrid_gmlp_moba_memory_decoder_step"


def rms_norm(x, g):
    xf = x.astype(jnp.float32)
    y = xf * lax.rsqrt(jnp.mean(xf * xf, axis=-1, keepdims=True) + NORM_EPS)
    return (y * g.astype(jnp.float32)).astype(x.dtype)


def layer_norm(x, g, b):
    xf = x.astype(jnp.float32)
    mu = jnp.mean(xf, axis=-1, keepdims=True)
    xc = xf - mu
    y = xc * lax.rsqrt(jnp.mean(xc * xc, axis=-1, keepdims=True) + NORM_EPS)
    return (y * g.astype(jnp.float32) + b.astype(jnp.float32)).astype(x.dtype)


def swiglu(x, w1, w3, w2):
    return (jax.nn.silu(x @ w1) * (x @ w3)) @ w2


def rope(x, pos):
    hd = x.shape[-1]
    half = hd // 2
    inv = ROPE_THETA ** (-jnp.arange(half, dtype=jnp.float32) * 2.0 / hd)
    ang = pos.astype(jnp.float32)[:, None] * inv[None, :]
    cos = jnp.cos(ang)[None, :, None, :].astype(x.dtype)
    sin = jnp.sin(ang)[None, :, None, :].astype(x.dtype)
    x1, x2 = x[..., :half], x[..., half:]
    return jnp.concatenate([x1 * cos - x2 * sin, x2 * cos + x1 * sin], axis=-1)


def chunk_spatial_gate(v, w, b):
    bsz, t_len, _ = v.shape
    n_c = -(-t_len // CHUNK)
    tp = n_c * CHUNK
    vr = jnp.pad(v, ((0, 0), (0, tp - t_len), (0, 0))).reshape(bsz, n_c, CHUNK, SGU_GROUPS, SGU_GROUP_DIM)
    wm = w * jnp.tril(jnp.ones((CHUNK, CHUNK), w.dtype))
    y = jnp.einsum('gij,bcjgd->bcigd', wm, vr) + b.T[None, None, :, :, None]
    return y.reshape(bsz, tp, SGU_DIM)[:, :t_len]


def moba_attention(q, q_pos, k_all, v_all):
    bsz, t_len, n_h, hd = q.shape
    l_len = k_all.shape[1]
    n_blk = -(-l_len // MOBA_BLOCK)
    pad = n_blk * MOBA_BLOCK - l_len
    k_blk = jnp.pad(k_all, ((0, 0), (0, pad), (0, 0), (0, 0))).reshape(bsz, n_blk, MOBA_BLOCK, n_h, hd).transpose(0, 3, 1, 2, 4)
    v_blk = jnp.pad(v_all, ((0, 0), (0, pad), (0, 0), (0, 0))).reshape(bsz, n_blk, MOBA_BLOCK, n_h, hd).transpose(0, 3, 1, 2, 4)
    k_mean = jnp.mean(k_blk.astype(jnp.float32), axis=3)
    n_sel = min(MOBA_TOPK, n_blk)
    scale = hd ** -0.5
    b_i = jnp.arange(bsz)[:, None, None, None]
    h_i = jnp.arange(n_h)[None, :, None, None]
    blk_ids = jnp.arange(n_blk)
    offs = jnp.arange(MOBA_BLOCK)

    def attend(args):
        qc, pc = args
        own = pc // MOBA_BLOCK
        gate = jnp.einsum('bthd,bhnd->bhtn', qc.astype(jnp.float32), k_mean)
        past = blk_ids[None, :] < own[:, None]
        gate = jnp.where(past[None, None], gate, -jnp.inf)
        _, sel = lax.top_k(gate, n_sel)
        own_b = jnp.broadcast_to(own[None, None, :, None], sel.shape[:-1] + (1,)).astype(sel.dtype)
        idx = jnp.concatenate([sel, own_b], axis=-1)
        slot_ok = jnp.concatenate([sel < own_b, jnp.ones(own_b.shape, bool)], axis=-1)
        kg = k_blk[b_i, h_i, idx]
        vg = v_blk[b_i, h_i, idx]
        key_pos = idx[..., None] * MOBA_BLOCK + offs
        ok = slot_ok[..., None] & (key_pos <= pc[None, None, :, None, None])
        s = jnp.einsum('bthd,bhtjsd->bhtjs', qc, kg).astype(jnp.float32) * scale
        s = jnp.where(ok, s, MASK_VALUE)
        p = jax.nn.softmax(s.reshape(s.shape[:3] + (-1,)), axis=-1).reshape(s.shape)
        return jnp.einsum('bhtjs,bhtjsd->bthd', p.astype(vg.dtype), vg)

    qc_len = min(MOBA_Q_CHUNK, t_len)
    n_c = -(-t_len // qc_len)
    tp = n_c * qc_len
    q_p = jnp.pad(q, ((0, 0), (0, tp - t_len), (0, 0), (0, 0)))
    pos_p = jnp.pad(q_pos, (0, tp - t_len), mode='edge')
    q_c = q_p.reshape(bsz, n_c, qc_len, n_h, hd).transpose(1, 0, 2, 3, 4)
    out = lax.map(attend, (q_c, pos_p.reshape(n_c, qc_len)))
    return out.transpose(1, 0, 2, 3, 4).reshape(bsz, tp, n_h, hd)[:, :t_len]


def cross_attention(q, k, v):
    scale = q.shape[-1] ** -0.5
    s = jnp.einsum('bthd,bmhd->bhtm', q, k).astype(jnp.float32) * scale
    p = jax.nn.softmax(s, axis=-1)
    return jnp.einsum('bhtm,bmhd->bthd', p.astype(v.dtype), v)


def memory_kv(mem, g, w):
    m = rms_norm(mem, g) @ w
    mk, mv = jnp.split(m, 2, axis=-1)
    shape = mem.shape[:2] + (MEM_HEADS, MEM_HEAD_DIM)
    return mk.reshape(shape), mv.reshape(shape)


def token_mixing(xn, pos, mem_k, mem_v, past_k, past_v, w_in, ln_g, ln_b, s_w, s_b, w_a, w_b, w_c, w_o):
    bsz, t_len, _ = xn.shape
    h = xn @ w_in
    a_u, a_v, b_q, b_k, b_v, c_q, gates = jnp.split(h, IN_SPLITS, axis=-1)
    u = jax.nn.gelu(a_u, approximate=False)
    v_n = layer_norm(jax.nn.gelu(a_v, approximate=False), ln_g, ln_b)
    y_a = u * chunk_spatial_gate(v_n, s_w, s_b)
    heads = (bsz, t_len, MOBA_HEADS, MOBA_HEAD_DIM)
    q = rope(b_q.reshape(heads), pos)
    k = rope(b_k.reshape(heads), pos)
    v = b_v.reshape(heads)
    if past_k is None:
        k_all, v_all = k, v
    else:
        k_all = jnp.concatenate([past_k, k], axis=1)
        v_all = jnp.concatenate([past_v, v], axis=1)
    y_b = moba_attention(q, pos, k_all, v_all).reshape(bsz, t_len, MOBA_DIM)
    y_c = cross_attention(c_q.reshape(bsz, t_len, MEM_HEADS, MEM_HEAD_DIM), mem_k, mem_v).reshape(bsz, t_len, MEM_DIM)
    g_a, g_b, g_c = jnp.split(jax.nn.sigmoid(gates), 3, axis=-1)
    merged = g_a * (y_a @ w_a) + g_b * (y_b @ w_b) + g_c * (y_c @ w_c)
    return merged @ w_o, k, v, v_n


def setup_inputs(seed: int = 0) -> dict:
    key = jax.random.key(seed)
    ks = iter(jax.random.split(key, 40))
    nrm = lambda shape, scale=1.0: jax.random.normal(next(ks), shape, jnp.float32) * scale
    gain = lambda shape: 1.0 + nrm(shape, 0.02)
    n_pages = PAST_LEN // PAGE_SIZE
    n_used = DEC_BATCH * n_pages
    n_pool = n_used + max(1, n_used // 4)
    kv_page = (DEPTH, n_pool, PAGE_SIZE, MOBA_HEADS, MOBA_HEAD_DIM)
    mem_shape = (DEPTH, DEC_BATCH, MEM_LEN, MEM_HEADS, MEM_HEAD_DIM)
    d_s = D_MODEL ** -0.5
    return {
        'x_prompt': nrm((BATCH, SEQ, D_MODEL)),
        'x_sample': nrm((DEC_BATCH, DEC_SEQ, D_MODEL)),
        'mem_prompt': nrm((BATCH, MEM_LEN, D_MODEL)),
        'cache_k': nrm(kv_page),
        'cache_v': nrm(kv_page),
        'cache_mem_k': nrm(mem_shape),
        'cache_mem_v': nrm(mem_shape),
        'page_table': jax.random.permutation(next(ks), n_pool)[:n_used].reshape(DEC_BATCH, n_pages).astype(jnp.int32),
        'ffn1_norm': gain((DEPTH, D_MODEL)),
        'ffn1_w1': nrm((DEPTH, D_MODEL, D_FF), d_s),
        'ffn1_w3': nrm((DEPTH, D_MODEL, D_FF), d_s),
        'ffn1_w2': nrm((DEPTH, D_FF, D_MODEL), D_FF ** -0.5),
        'mix_norm': gain((DEPTH, D_MODEL)),
        'w_in': nrm((DEPTH, D_MODEL, IN_COLS), d_s),
        'sgu_ln_g': gain((DEPTH, SGU_DIM)),
        'sgu_ln_b': nrm((DEPTH, SGU_DIM), 0.02),
        'sgu_w': nrm((DEPTH, SGU_GROUPS, CHUNK, CHUNK), CHUNK ** -0.5),
        'sgu_b': 1.0 + nrm((DEPTH, SGU_GROUPS, CHUNK), 0.1),
        'mem_norm': gain((DEPTH, D_MODEL)),
        'w_mem_kv': nrm((DEPTH, D_MODEL, 2 * MEM_DIM), d_s),
        'w_a_out': nrm((DEPTH, SGU_DIM, D_MODEL), SGU_DIM ** -0.5),
        'w_b_out': nrm((DEPTH, MOBA_DIM, D_MODEL), MOBA_DIM ** -0.5),
        'w_c_out': nrm((DEPTH, MEM_DIM, D_MODEL), MEM_DIM ** -0.5),
        'w_o': nrm((DEPTH, D_MODEL, D_MODEL), d_s),
        'ffn2_norm': gain((DEPTH, D_MODEL)),
        'ffn2_w1': nrm((DEPTH, D_MODEL, D_FF), d_s),
        'ffn2_w3': nrm((DEPTH, D_MODEL, D_FF), d_s),
        'ffn2_w2': nrm((DEPTH, D_FF, D_MODEL), D_FF ** -0.5),
        'final_norm': gain((D_MODEL,)),
    }


def reference(x_prompt, x_sample, mem_prompt, cache_k, cache_v, cache_mem_k, cache_mem_v, page_table,
              ffn1_norm, ffn1_w1, ffn1_w3, ffn1_w2, mix_norm, w_in, sgu_ln_g, sgu_ln_b, sgu_w, sgu_b,
              mem_norm, w_mem_kv, w_a_out, w_b_out, w_c_out, w_o, ffn2_norm, ffn2_w1, ffn2_w3, ffn2_w2,
              final_norm):
    def layer(x, l, pos, mem_k, mem_v, past_k, past_v):
        x = x + 0.5 * swiglu(rms_norm(x, ffn1_norm[l]), ffn1_w1[l], ffn1_w3[l], ffn1_w2[l])
        mixed, k_new, v_new, sgu_v = token_mixing(
            rms_norm(x, mix_norm[l]), pos, mem_k, mem_v, past_k, past_v, w_in[l], sgu_ln_g[l], sgu_ln_b[l],
            sgu_w[l], sgu_b[l], w_a_out[l], w_b_out[l], w_c_out[l], w_o[l])
        x = x + mixed
        x = x + 0.5 * swiglu(rms_norm(x, ffn2_norm[l]), ffn2_w1[l], ffn2_w3[l], ffn2_w2[l])
        return x, k_new, v_new, sgu_v

    pos_p = jnp.arange(x_prompt.shape[1], dtype=jnp.int32)
    h = x_prompt
    kp, vp, mkp, mvp = [], [], [], []
    for l in range(DEPTH):
        mk, mv = memory_kv(mem_prompt, mem_norm[l], w_mem_kv[l])
        h, k_n, v_n, _ = layer(h, l, pos_p, mk, mv, None, None)
        kp.append(k_n)
        vp.append(v_n)
        mkp.append(mk)
        mvp.append(mv)
    y_prompt = rms_norm(h, final_norm)

    dec_b, n_pages = page_table.shape
    past_len = n_pages * cache_k.shape[2]
    pos_s = past_len + jnp.arange(x_sample.shape[1], dtype=jnp.int32)
    h = x_sample
    ks_, vs_, sv_ = [], [], []
    for l in range(DEPTH):
        past_k = cache_k[l, page_table].reshape(dec_b, past_len, MOBA_HEADS, MOBA_HEAD_DIM)
        past_v = cache_v[l, page_table].reshape(dec_b, past_len, MOBA_HEADS, MOBA_HEAD_DIM)
        h, k_n, v_n, s_v = layer(h, l, pos_s, cache_mem_k[l], cache_mem_v[l], past_k, past_v)
        ks_.append(k_n)
        vs_.append(v_n)
        sv_.append(s_v)
    y_sample = rms_norm(h, final_norm)

    new_k_prompt = jnp.stack(kp)
    new_v_prompt = jnp.stack(vp)
    new_mem_k_prompt = jnp.stack(mkp)
    new_mem_v_prompt = jnp.stack(mvp)
    new_k_sample = jnp.stack(ks_)
    new_v_sample = jnp.stack(vs_)
    new_sgu_v_sample = jnp.stack(sv_)
    return (y_prompt, y_sample, new_k_prompt, new_v_prompt, new_mem_k_prompt, new_mem_v_prompt, new_k_sample, new_v_sample, new_sgu_v_sample)
```

```python
import functools

import jax
import jax.numpy as jnp
import numpy as np
from jax import lax
from jax.experimental import pallas as pl
from jax.experimental.pallas import tpu as pltpu

F32 = jnp.float32
BF16 = jnp.bfloat16

D_MODEL = 1024
SGU_GROUPS = 4
SGU_GROUP_DIM = 128
SGU_DIM = 512
CHUNK = 128
MOBA_HEADS = 8
MOBA_HEAD_DIM = 64
MOBA_DIM = 512
MOBA_BLOCK = 256
MOBA_TOPK = 3
MEM_HEADS = 4
MEM_HEAD_DIM = 128
MEM_DIM = 512
D_FF = 2816
ROPE_THETA = 10000.0
NORM_EPS = 1e-6
MASK_VALUE = -1e30
IN_COLS = 6144
GATE_COL0 = 3072

V7X_VMEM_BYTES = 64 * 1024 * 1024
LANES = 128
SUBLANES = 8
SAMPLE_ROWS = 8
TEMP_ALLOWANCE = 12 * 1024 * 1024

NT_DIMS = (((1,), (1,)), ((), ()))


def _vmem_limit(block_bytes):
    return int(min(block_bytes + TEMP_ALLOWANCE, V7X_VMEM_BYTES - 4 * 1024 * 1024))


def _nbytes(shape, dtype):
    return int(np.prod(shape)) * jnp.dtype(dtype).itemsize


def _resident(shape):
    nd = len(shape)
    return pl.BlockSpec(shape, lambda *_: (0,) * nd, pipeline_mode=pl.Buffered(1))


def _dot(a, b):
    return jnp.dot(a, b, preferred_element_type=F32)


def _dot_nt(a, b, precision=None):
    return lax.dot_general(a, b, NT_DIMS, precision=precision, preferred_element_type=F32)


def _rms(x, g):
    return x * lax.rsqrt(jnp.mean(x * x, axis=-1, keepdims=True) + NORM_EPS) * g


def _sigmoid(x):
    return 1.0 / (1.0 + jnp.exp(-x))


def _gelu(x):
    return 0.5 * x * (1.0 + lax.erf(x * np.float32(np.sqrt(0.5))))


def _softmax_rows(s):
    p = jnp.exp(s - jnp.max(s, axis=-1, keepdims=True))
    return p, 1.0 / jnp.sum(p, axis=-1, keepdims=True)


_FF_CHUNK = 512
_FF_CHUNKS = tuple((s, min(_FF_CHUNK, D_FF - s)) for s in range(0, D_FF, _FF_CHUNK))


def _ffn_kernel(x_ref, g_ref, w1_ref, w3_ref, w2_ref, fg_ref, o_ref, a_ref, *, final_norm):
    x = x_ref[...]
    xb = _rms(x, g_ref[...]).astype(BF16)
    for s, w in _FF_CHUNKS:
        h1 = _dot(xb, w1_ref[:, s:s + w])
        h3 = _dot(xb, w3_ref[:, s:s + w])
        a_ref[:, s:s + w] = (h1 * _sigmoid(h1) * h3).astype(BF16)
    y = x + 0.5 * _dot(a_ref[...], w2_ref[...])
    if final_norm:
        y = _rms(y, fg_ref[...])
    o_ref[...] = y


def _ffn(x, g, w1, w3, w2, fg, *, tm, final_norm):
    m = x.shape[0]
    blocks = (2 * 2 * _nbytes((tm, D_MODEL), F32) + 3 * _nbytes((D_MODEL, D_FF), BF16)
              + _nbytes((tm, D_FF), BF16))
    return pl.pallas_call(
        functools.partial(_ffn_kernel, final_norm=final_norm),
        grid=(m // tm,),
        in_specs=[
            pl.BlockSpec((tm, D_MODEL), lambda i: (i, 0)),
            _resident((1, D_MODEL)),
            _resident((D_MODEL, D_FF)),
            _resident((D_MODEL, D_FF)),
            _resident((D_FF, D_MODEL)),
            _resident((1, D_MODEL)),
        ],
        out_specs=pl.BlockSpec((tm, D_MODEL), lambda i: (i, 0)),
        out_shape=jax.ShapeDtypeStruct((m, D_MODEL), F32),
        scratch_shapes=[pltpu.VMEM((tm, D_FF), BF16)],
        compiler_params=pltpu.CompilerParams(
            dimension_semantics=("parallel",), vmem_limit_bytes=_vmem_limit(blocks)),
        name="ffn",
    )(x, g, w1, w3, w2, fg)


def _rope(h, cos, sin_lo, sin_hi):
    outs = []
    for c in range(h.shape[1] // LANES):
        s = h[:, c * LANES:(c + 1) * LANES]
        outs.append(s * cos + pltpu.roll(s, LANES - 32, 1) * sin_lo + pltpu.roll(s, 32, 1) * sin_hi)
    return jnp.concatenate(outs, axis=1)


def _inproj_kernel(x_ref, g_ref, w_ref, lng_ref, lnb_ref, cos_ref, slo_ref, shi_ref,
                   u_ref, vn_ref, q_ref, k_ref, v_ref, cq_ref, gate_ref):
    xb = _rms(x_ref[...], g_ref[...]).astype(BF16)

    def proj(c0, w=512):
        return _dot(xb, w_ref[:, c0:c0 + w])

    u_ref[...] = _gelu(proj(0))
    a = _gelu(proj(512))
    ac = a - jnp.mean(a, axis=-1, keepdims=True)
    vn_ref[...] = (ac * lax.rsqrt(jnp.mean(ac * ac, axis=-1, keepdims=True) + NORM_EPS)
                   * lng_ref[...] + lnb_ref[...])
    cos, slo, shi = cos_ref[...], slo_ref[...], shi_ref[...]
    q_ref[...] = _rope(proj(1024), cos, slo, shi)
    k_ref[...] = _rope(proj(1536), cos, slo, shi)
    v_ref[...] = proj(2048)
    cq_ref[...] = proj(2560)
    for c in range(6):
        gate_ref[:, c * 512:(c + 1) * 512] = _sigmoid(proj(GATE_COL0 + c * 512))


def _inproj(x, g, w, lng, lnb, cos, slo, shi, *, tm):
    m = x.shape[0]
    n_tab = cos.shape[0] // tm
    out_cols = (512,) * 6 + (3 * D_MODEL,)
    blocks = (2 * _nbytes((tm, D_MODEL), F32) + _nbytes((D_MODEL, IN_COLS), BF16)
              + 2 * _nbytes((tm, IN_COLS), F32) + 6 * _nbytes((tm, LANES), F32))
    tab = pl.BlockSpec((tm, LANES), lambda i: (i % n_tab, 0))
    return pl.pallas_call(
        _inproj_kernel,
        grid=(m // tm,),
        in_specs=[
            pl.BlockSpec((tm, D_MODEL), lambda i: (i, 0)),
            _resident((1, D_MODEL)),
            _resident((D_MODEL, IN_COLS)),
            _resident((1, SGU_DIM)),
            _resident((1, SGU_DIM)),
            tab, tab, tab,
        ],
        out_specs=[pl.BlockSpec((tm, c), lambda i: (i, 0)) for c in out_cols],
        out_shape=[jax.ShapeDtypeStruct((m, c), F32) for c in out_cols],
        compiler_params=pltpu.CompilerParams(
            dimension_semantics=("parallel",), vmem_limit_bytes=_vmem_limit(blocks)),
        name="inproj",
    )(x, g, w, lng, lnb, cos, slo, shi)


def _rope_tables(pos):
    half = MOBA_HEAD_DIM // 2
    inv = ROPE_THETA ** (-jnp.arange(half, dtype=F32) * 2.0 / MOBA_HEAD_DIM)
    ang = pos.astype(F32)[:, None] * inv[None, :]
    cos, sin = jnp.tile(jnp.cos(ang), (1, 4)), jnp.tile(jnp.sin(ang), (1, 4))
    first_half = (jnp.arange(LANES) % MOBA_HEAD_DIM) < half
    return cos, jnp.where(first_half, -sin, 0.0), jnp.where(first_half, 0.0, sin)


def _memkv_kernel(m_ref, g_ref, w_ref, k_ref, v_ref):
    xb = _rms(m_ref[...], g_ref[...]).astype(BF16)
    k_ref[...] = _dot(xb, w_ref[:, :MEM_DIM])
    v_ref[...] = _dot(xb, w_ref[:, MEM_DIM:])


def _memkv(mem, g, w, *, tm):
    m = mem.shape[0]
    blocks = (2 * _nbytes((tm, D_MODEL), F32) + _nbytes((D_MODEL, 2 * MEM_DIM), BF16)
              + 4 * _nbytes((tm, MEM_DIM), F32))
    return pl.pallas_call(
        _memkv_kernel,
        grid=(m // tm,),
        in_specs=[pl.BlockSpec((tm, D_MODEL), lambda i: (i, 0)), _resident((1, D_MODEL)),
                  _resident((D_MODEL, 2 * MEM_DIM))],
        out_specs=[pl.BlockSpec((tm, MEM_DIM), lambda i: (i, 0))] * 2,
        out_shape=[jax.ShapeDtypeStruct((m, MEM_DIM), F32)] * 2,
        compiler_params=pltpu.CompilerParams(
            dimension_semantics=("parallel",), vmem_limit_bytes=_vmem_limit(blocks)),
        name="memkv",
    )(mem, g, w)


def _topk_select(g, n_past, n_rows):
    row = lax.broadcasted_iota(jnp.int32, g.shape, 0)
    rank = jnp.zeros(g.shape, F32)
    for m in range(n_past):
        gm = g[m:m + 1, :]
        beats = (gm > g) | ((gm == g) & (row > m))
        rank = rank + beats.astype(F32)
    return (row < n_past) & (rank < MOBA_TOPK)


def _moba_prompt_kernel(q_ref, k_ref, v_ref, o_ref):
    t_len = q_ref.shape[1]
    n_blk = t_len // MOBA_BLOCK
    k = k_ref[0]
    kb = k.astype(BF16)
    vb = v_ref[0].astype(BF16)
    k_mean = jnp.sum(k.reshape(n_blk, MOBA_BLOCK, LANES), axis=1) * (1.0 / MOBA_BLOCK)
    lane = lax.broadcasted_iota(jnp.int32, (1, LANES), 1)
    head_masks = (lane < MOBA_HEAD_DIM, lane >= MOBA_HEAD_DIM)
    r_i = lax.broadcasted_iota(jnp.int32, (MOBA_BLOCK, MOBA_BLOCK), 0)
    c_i = lax.broadcasted_iota(jnp.int32, (MOBA_BLOCK, MOBA_BLOCK), 1)
    causal = c_i <= r_i
    scale = MOBA_HEAD_DIM ** -0.5

    for qi in range(n_blk):
        q_blk = q_ref[0, qi * MOBA_BLOCK:(qi + 1) * MOBA_BLOCK, :]
        t_k = (qi + 1) * MOBA_BLOCK
        outs = []
        for hm in head_masks:
            qm = jnp.where(hm, q_blk, 0.0)
            s = _dot_nt((qm * scale).astype(BF16), kb[:t_k])
            parts = []
            if qi > MOBA_TOPK:
                gate = _dot_nt(k_mean, qm, precision=lax.Precision.HIGHEST)
                sel = _topk_select(gate, qi, n_blk).astype(F32)
                sel_t = jnp.concatenate(
                    [sel, jnp.zeros((LANES - n_blk, MOBA_BLOCK), F32)], axis=0).T
                for n in range(qi):
                    keep = sel_t[:, n:n + 1] > 0.5
                    parts.append(jnp.where(keep, s[:, n * MOBA_BLOCK:(n + 1) * MOBA_BLOCK], MASK_VALUE))
            else:
                for n in range(qi):
                    parts.append(s[:, n * MOBA_BLOCK:(n + 1) * MOBA_BLOCK])
            parts.append(jnp.where(causal, s[:, qi * MOBA_BLOCK:t_k], MASK_VALUE))
            sm = jnp.concatenate(parts, axis=1) if len(parts) > 1 else parts[0]
            p, inv_l = _softmax_rows(sm)
            outs.append(_dot(p.astype(BF16), vb[:t_k]) * inv_l)
        o_ref[0, qi * MOBA_BLOCK:(qi + 1) * MOBA_BLOCK, :] = jnp.where(head_masks[0], outs[0], outs[1])


def _moba_prompt(q, k, v):
    bsz, t_len, _ = q.shape
    spec = pl.BlockSpec((1, t_len, LANES), lambda b, h: (b, 0, h))
    blocks = (8 * _nbytes((t_len, LANES), F32) + 2 * _nbytes((t_len, LANES), BF16)
              + 8 * _nbytes((MOBA_BLOCK, t_len), F32))
    return pl.pallas_call(
        _moba_prompt_kernel,
        grid=(bsz, MOBA_DIM // LANES),
        in_specs=[spec, spec, spec],
        out_specs=spec,
        out_shape=jax.ShapeDtypeStruct((bsz, t_len, MOBA_DIM), F32),
        compiler_params=pltpu.CompilerParams(
            dimension_semantics=("parallel", "parallel"), vmem_limit_bytes=_vmem_limit(blocks)),
        name="moba_prompt",
    )(q, k, v)


def _merge(x, ya, yb, yc, gate_ref, wa_ref, wb_ref, wc_ref, wo_ref):
    t = (gate_ref[:, 0:D_MODEL] * _dot(ya.astype(BF16), wa_ref[...])
         + gate_ref[:, D_MODEL:2 * D_MODEL] * _dot(yb.astype(BF16), wb_ref[...])
         + gate_ref[:, 2 * D_MODEL:3 * D_MODEL] * _dot(yc.astype(BF16), wc_ref[...]))
    return x + _dot(t.astype(BF16), wo_ref[...])


def _mem_attention(cq, mkb, mvb):
    scale = MEM_HEAD_DIM ** -0.5
    outs = []
    for h in range(MEM_HEADS):
        sl = slice(h * MEM_HEAD_DIM, (h + 1) * MEM_HEAD_DIM)
        s = _dot_nt(cq[:, sl].astype(BF16), mkb[:, sl]) * scale
        p, inv_l = _softmax_rows(s)
        outs.append(_dot(p.astype(BF16), mvb[:, sl]) * inv_l)
    return jnp.concatenate(outs, axis=1)


def _mix_prompt_kernel(u_ref, vn_ref, cq_ref, gate_ref, yb_ref, x_ref, mk_ref, mv_ref,
                       sw_ref, sb_ref, wa_ref, wb_ref, wc_ref, wo_ref, o_ref):
    tm = u_ref.shape[0]
    r_i = lax.broadcasted_iota(jnp.int32, (CHUNK, CHUNK), 0)
    c_i = lax.broadcasted_iota(jnp.int32, (CHUNK, CHUNK), 1)
    wm = [jnp.where(c_i <= r_i, sw_ref[g], 0.0).astype(BF16) for g in range(SGU_GROUPS)]
    rows = []
    for ch in range(tm // CHUNK):
        rs = slice(ch * CHUNK, (ch + 1) * CHUNK)
        cols = []
        for g in range(SGU_GROUPS):
            cs = slice(g * SGU_GROUP_DIM, (g + 1) * SGU_GROUP_DIM)
            y = _dot(wm[g], vn_ref[rs, cs].astype(BF16)) + sb_ref[:, g:g + 1]
            cols.append(u_ref[rs, cs] * y)
        rows.append(jnp.concatenate(cols, axis=1))
    ya = jnp.concatenate(rows, axis=0)
    yc = _mem_attention(cq_ref[...], mk_ref[0].astype(BF16), mv_ref[0].astype(BF16))
    o_ref[...] = _merge(x_ref[...], ya, yb_ref[...], yc, gate_ref, wa_ref, wb_ref, wc_ref, wo_ref)


def _mix_prompt(u, vn, cq, gate, yb, x, mk, mv, sw, sb_t, wa, wb, wc, wo, *, tm, t_len):
    m = x.shape[0]
    per_b = t_len // tm

    def rows(c):
        return pl.BlockSpec((tm, c), lambda i: (i, 0))

    mem_spec = pl.BlockSpec((1,) + mk.shape[1:], lambda i: (i // per_b, 0, 0))
    blocks = (2 * _nbytes((tm, 4 * 512 + 3 * D_MODEL + 2 * D_MODEL), F32)
              + 4 * _nbytes(mk.shape[1:], F32) + _nbytes((3 * 512 + D_MODEL, D_MODEL), BF16))
    return pl.pallas_call(
        _mix_prompt_kernel,
        grid=(m // tm,),
        in_specs=[rows(512), rows(512), rows(512), rows(3 * D_MODEL), rows(512), rows(D_MODEL),
                  mem_spec, mem_spec,
                  _resident(sw.shape), _resident(sb_t.shape),
                  _resident(wa.shape), _resident(wb.shape), _resident(wc.shape), _resident(wo.shape)],
        out_specs=rows(D_MODEL),
        out_shape=jax.ShapeDtypeStruct((m, D_MODEL), F32),
        compiler_params=pltpu.CompilerParams(
            dimension_semantics=("parallel",), vmem_limit_bytes=_vmem_limit(blocks)),
        name="mix_prompt",
    )(u, vn, cq, gate, yb, x, mk, mv, sw, sb_t, wa, wb, wc, wo)


def _premix_sample_kernel(u_ref, vn_ref, cq_ref, mk_ref, mv_ref, wd_ref, bl_ref, ya_ref, yc_ref):
    vn = vn_ref[...]
    y = bl_ref[...] + wd_ref[0] * vn
    for d in range(1, wd_ref.shape[0]):
        y = y + wd_ref[d] * pltpu.roll(vn, d, 0)
    ya_ref[...] = u_ref[...] * y

    lane = lax.broadcasted_iota(jnp.int32, (1, MEM_DIM), 1) // MEM_HEAD_DIM
    scale = MEM_HEAD_DIM ** -0.5
    for b in range(mk_ref.shape[0]):
        rs = slice(b * SAMPLE_ROWS, (b + 1) * SAMPLE_ROWS)
        q8 = cq_ref[rs, :]
        q_exp = jnp.concatenate([jnp.where(lane == h, q8, 0.0) for h in range(MEM_HEADS)], axis=0)
        s = _dot_nt(q_exp.astype(BF16), mk_ref[b].astype(BF16)) * scale
        p, inv_l = _softmax_rows(s)
        o = _dot(p.astype(BF16), mv_ref[b].astype(BF16)) * inv_l
        yc = jnp.zeros((SAMPLE_ROWS, MEM_DIM), F32)
        for h in range(MEM_HEADS):
            yc = yc + jnp.where(lane == h, o[h * SAMPLE_ROWS:(h + 1) * SAMPLE_ROWS, :], 0.0)
        yc_ref[rs, :] = yc


def _premix_sample(u, vn, cq, mk, mv, wd, bl, *, n_b):
    m = u.shape[0]
    tm = n_b * SAMPLE_ROWS
    rows = pl.BlockSpec((tm, 512), lambda i: (i, 0))
    mem_spec = pl.BlockSpec((n_b,) + mk.shape[1:], lambda i: (i, 0, 0))
    blocks = (2 * 5 * _nbytes((tm, 512), F32) + 4 * _nbytes((n_b,) + mk.shape[1:], F32)
              + _nbytes(wd.shape, F32) + _nbytes(bl.shape, F32))
    return pl.pallas_call(
        _premix_sample_kernel,
        grid=(m // tm,),
        in_specs=[rows, rows, rows, mem_spec, mem_spec, _resident(wd.shape), _resident(bl.shape)],
        out_specs=[rows, rows],
        out_shape=[jax.ShapeDtypeStruct((m, 512), F32)] * 2,
        compiler_params=pltpu.CompilerParams(
            dimension_semantics=("parallel",), vmem_limit_bytes=_vmem_limit(blocks)),
        name="premix_sample",
    )(u, vn, cq, mk, mv, wd, bl)


def _merge_kernel(x_ref, ya_ref, yb_ref, yc_ref, gate_ref, wa_ref, wb_ref, wc_ref, wo_ref, o_ref):
    o_ref[...] = _merge(x_ref[...], ya_ref[...], yb_ref[...], yc_ref[...], gate_ref,
                        wa_ref, wb_ref, wc_ref, wo_ref)


def _merge_sample(x, ya, yb, yc, gate, wa, wb, wc, wo):
    m = x.shape[0]
    args = (x, ya, yb, yc, gate, wa, wb, wc, wo)
    blocks = sum(_nbytes(a.shape, a.dtype) for a in args) + _nbytes(x.shape, F32)
    return pl.pallas_call(
        _merge_kernel,
        grid=(1,),
        in_specs=[_resident(a.shape) for a in args],
        out_specs=pl.BlockSpec((m, D_MODEL), lambda i: (0, 0)),
        out_shape=jax.ShapeDtypeStruct((m, D_MODEL), F32),
        compiler_params=pltpu.CompilerParams(
            dimension_semantics=("arbitrary",), vmem_limit_bytes=_vmem_limit(blocks)),
        name="merge_sample",
    )(*args)


_PAGES_PER_CHUNK = 8


def _moba_sample_kernel(pt_ref, q_ref, kn_ref, vnew_ref, ck_hbm, cv_hbm, o_ref,
                        buf, sem, s_ref, *, page_base, n_pages, page_size, n_q):
    b = pl.program_id(0)
    n_req = pl.num_programs(0)
    chunk_keys = _PAGES_PER_CHUNK * page_size
    n_chunks = n_pages // _PAGES_PER_CHUNK
    blocks_per_chunk = chunk_keys // MOBA_BLOCK
    n_past_blk = n_chunks * blocks_per_chunk
    past_len = n_pages * page_size
    q_rows = n_q * MOBA_HEADS

    def copies(req, c, slot):
        src = ck_hbm if c < n_chunks else cv_hbm
        first = req * n_pages + (c % n_chunks) * _PAGES_PER_CHUNK
        return [pltpu.make_async_copy(src.at[pt_ref[first + p] + page_base],
                                      buf.at[slot, :, pl.ds(p * page_size, page_size)],
                                      sem.at[slot])
                for p in range(_PAGES_PER_CHUNK)]

    def start(req, c, slot):
        for cp in copies(req, c, slot):
            cp.start()

    def wait(req, c, slot):
        for cp in copies(req, c, slot):
            cp.wait()

    @pl.when(b == 0)
    def _():
        start(b, 0, 0)

    sub_i = lax.broadcasted_iota(jnp.int32, (SAMPLE_ROWS, MOBA_DIM), 0)
    own_head = sub_i == lax.broadcasted_iota(jnp.int32, (SAMPLE_ROWS, MOBA_DIM), 1) // MOBA_HEAD_DIM
    q8 = q_ref[...] * (MOBA_HEAD_DIM ** -0.5)
    q_exp = jnp.concatenate(
        [jnp.where(own_head, jnp.broadcast_to(q8[t:t + 1, :], (MOBA_HEADS, MOBA_DIM)), 0.0)
         for t in range(n_q)], axis=0)
    q_exp_b = q_exp.astype(BF16)

    def pad_rows(x8):
        return jnp.concatenate([x8, jnp.zeros((LANES - SAMPLE_ROWS, x8.shape[1]), F32)], axis=0)

    blk_lane = lax.broadcasted_iota(jnp.int32, (MOBA_DIM, LANES), 1)
    ksum = jnp.zeros((MOBA_DIM, LANES), F32)
    acc = jnp.zeros((q_rows, MOBA_DIM), F32)
    inv_l = None
    for c in range(2 * n_chunks):
        slot = c % 2
        wait(b, c, slot)
        if c + 1 < 2 * n_chunks:
            start(b, c + 1, 1 - slot)
        else:
            @pl.when(b + 1 < n_req)
            def _():
                start(b + 1, 0, 1 - slot)

        if c < n_chunks:
            kc = buf[slot]
            for j in range(blocks_per_chunk):
                col = jnp.sum(kc[:, j * MOBA_BLOCK:(j + 1) * MOBA_BLOCK], axis=1, keepdims=True)
                ksum = jnp.where(blk_lane == c * blocks_per_chunk + j, col, ksum)
            s_ref[:, c * chunk_keys:(c + 1) * chunk_keys] = _dot(q_exp_b, kc.astype(BF16))
            if c == n_chunks - 1:
                gate = jnp.dot(q_exp, ksum * (1.0 / MOBA_BLOCK), precision=lax.Precision.HIGHEST,
                               preferred_element_type=F32)
                lane = lax.broadcasted_iota(jnp.int32, gate.shape, 1)
                rank = jnp.zeros(gate.shape, F32)
                for m in range(n_past_blk):
                    gm = gate[:, m:m + 1]
                    beats = (gm > gate) | ((gm == gate) & (lane > m))
                    rank = rank + beats.astype(F32)
                sel = ((lane < n_past_blk) & (rank < MOBA_TOPK)).astype(F32)
                s_own = _dot_nt(q_exp_b, pad_rows(kn_ref[...]).astype(BF16))
                t_i = lax.broadcasted_iota(jnp.int32, s_own.shape, 0) // MOBA_HEADS
                s_own = jnp.where(lane <= t_i, s_own, MASK_VALUE)
                mx_acc = jnp.full((q_rows, MOBA_BLOCK), MASK_VALUE, F32)
                for n in range(n_past_blk):
                    cs = slice(n * MOBA_BLOCK, (n + 1) * MOBA_BLOCK)
                    blk = jnp.where(sel[:, n:n + 1] > 0.5, s_ref[:, cs], MASK_VALUE)
                    s_ref[:, cs] = blk
                    mx_acc = jnp.maximum(mx_acc, blk)
                mx = jnp.maximum(jnp.max(mx_acc, axis=1, keepdims=True),
                                 jnp.max(s_own, axis=1, keepdims=True))
                p_own = jnp.exp(s_own - mx)
                l_acc = jnp.zeros((q_rows, MOBA_BLOCK), F32)
                for n in range(n_past_blk):
                    cs = slice(n * MOBA_BLOCK, (n + 1) * MOBA_BLOCK)
                    blk = jnp.exp(s_ref[:, cs] - mx)
                    s_ref[:, cs] = blk
                    l_acc = l_acc + blk
                s_ref[:, past_len:past_len + LANES] = p_own
                inv_l = 1.0 / (jnp.sum(l_acc, axis=1, keepdims=True)
                               + jnp.sum(p_own, axis=1, keepdims=True))
        else:
            cc = c - n_chunks
            p_c = s_ref[:, cc * chunk_keys:(cc + 1) * chunk_keys].astype(BF16)
            acc = acc + _dot_nt(p_c, buf[slot].astype(BF16))

    p_own = s_ref[:, past_len:past_len + LANES].astype(BF16)
    acc = (acc + _dot(p_own, pad_rows(vnew_ref[...]).astype(BF16))) * inv_l
    y = jnp.zeros((SAMPLE_ROWS, MOBA_DIM), F32)
    for t in range(n_q):
        grp = jnp.where(own_head, acc[t * MOBA_HEADS:(t + 1) * MOBA_HEADS, :], 0.0)
        y = jnp.where(sub_i == t, jnp.sum(grp, axis=0, keepdims=True), y)
    o_ref[...] = y


def _moba_sample(page_table, q, k_new, v_new, cache_k, cache_v, *, layer, n_q):
    n_req, n_pages = page_table.shape
    depth, n_pool, page_size = cache_k.shape[:3]
    ck = jnp.transpose(cache_k, (0, 1, 3, 4, 2)).reshape(depth * n_pool, MOBA_DIM, page_size)
    cv = jnp.transpose(cache_v, (0, 1, 3, 4, 2)).reshape(depth * n_pool, MOBA_DIM, page_size)
    chunk_keys = _PAGES_PER_CHUNK * page_size
    past_len = n_pages * page_size
    q_rows = n_q * MOBA_HEADS
    rows = pl.BlockSpec((SAMPLE_ROWS, MOBA_DIM), lambda b, pt: (b, 0))
    hbm = pl.BlockSpec(memory_space=pl.ANY)
    scratch = [pltpu.VMEM((2, MOBA_DIM, chunk_keys), F32),
               pltpu.SemaphoreType.DMA((2,)),
               pltpu.VMEM((q_rows, past_len + LANES), F32)]
    blocks = (_nbytes((2, MOBA_DIM, chunk_keys), F32) + _nbytes((q_rows, past_len + LANES), F32)
              + 8 * _nbytes((SAMPLE_ROWS, MOBA_DIM), F32))
    return pl.pallas_call(
        functools.partial(_moba_sample_kernel, page_base=layer * n_pool, n_pages=n_pages,
                          page_size=page_size, n_q=n_q),
        grid_spec=pltpu.PrefetchScalarGridSpec(
            num_scalar_prefetch=1,
            grid=(n_req,),
            in_specs=[rows, rows, rows, hbm, hbm],
            out_specs=rows,
            scratch_shapes=scratch),
        out_shape=jax.ShapeDtypeStruct((n_req * SAMPLE_ROWS, MOBA_DIM), F32),
        compiler_params=pltpu.CompilerParams(
            dimension_semantics=("arbitrary",), vmem_limit_bytes=_vmem_limit(blocks)),
        name="moba_sample",
    )(page_table.reshape(-1), q, k_new, v_new, ck, cv)


def _sample_sgu_tables(sgu_w, sgu_b, n_b, t_new):
    t = jnp.arange(SAMPLE_ROWS)
    d = jnp.arange(t_new)
    src = t[None, :] - d[:, None]
    ok = (src >= 0) & (t[None, :] < t_new)
    w_td = sgu_w[:, t[None, :].clip(0, t_new - 1), src.clip(0, t_new - 1)]
    w_td = jnp.where(ok[None], w_td, 0.0)
    wd = jnp.repeat(jnp.transpose(w_td, (1, 2, 0)), SGU_GROUP_DIM, axis=2)
    bl = jnp.where((t < t_new)[:, None],
                   jnp.repeat(sgu_b[:, t.clip(0, t_new - 1)].T, SGU_GROUP_DIM, axis=1), 0.0)
    return jnp.tile(wd, (1, n_b, 1)), jnp.tile(bl, (n_b, 1))


def kernel(x_prompt, x_sample, mem_prompt, cache_k, cache_v, cache_mem_k, cache_mem_v, page_table,
           ffn1_norm, ffn1_w1, ffn1_w3, ffn1_w2, mix_norm, w_in, sgu_ln_g, sgu_ln_b, sgu_w, sgu_b,
           mem_norm, w_mem_kv, w_a_out, w_b_out, w_c_out, w_o, ffn2_norm, ffn2_w1, ffn2_w3, ffn2_w2,
           final_norm):
    depth = w_in.shape[0]
    bsz, t_len, _ = x_prompt.shape
    dec_b, t_new, _ = x_sample.shape
    n_pages = page_table.shape[1]
    page_size = cache_k.shape[2]
    past_len = n_pages * page_size
    mem_len = mem_prompt.shape[1]
    assert t_len % 512 == 0 and t_new <= SAMPLE_ROWS and past_len % MOBA_BLOCK == 0
    assert n_pages % _PAGES_PER_CHUNK == 0 and (_PAGES_PER_CHUNK * page_size) % MOBA_BLOCK == 0

    row = lambda a: a.reshape(1, -1)
    bf = lambda a: a.astype(BF16)
    fin = row(final_norm)

    tm_p = 512
    cos_p, slo_p, shi_p = _rope_tables(jnp.arange(t_len, dtype=jnp.int32))
    h = x_prompt.reshape(bsz * t_len, D_MODEL)
    mem = mem_prompt.reshape(bsz * mem_len, D_MODEL)
    kp, vp, mkp, mvp = [], [], [], []
    layer_w = []
    for l in range(depth):
        layer_w.append(dict(
            f1=(row(ffn1_norm[l]), bf(ffn1_w1[l]), bf(ffn1_w3[l]), bf(ffn1_w2[l])),
            f2=(row(ffn2_norm[l]), bf(ffn2_w1[l]), bf(ffn2_w3[l]), bf(ffn2_w2[l])),
            win=bf(w_in[l]), wa=bf(w_a_out[l]), wb=bf(w_b_out[l]), wc=bf(w_c_out[l]), wo=bf(w_o[l])))
    for l in range(depth):
        lw = layer_w[l]
        mk, mv = _memkv(mem, row(mem_norm[l]), bf(w_mem_kv[l]), tm=512)
        h = _ffn(h, *lw["f1"], fin, tm=tm_p, final_norm=False)
        u, vn, q, k, v, cq, gate = _inproj(h, row(mix_norm[l]), lw["win"], row(sgu_ln_g[l]),
                                           row(sgu_ln_b[l]), cos_p, slo_p, shi_p, tm=tm_p)
        yb = _moba_prompt(q.reshape(bsz, t_len, MOBA_DIM), k.reshape(bsz, t_len, MOBA_DIM),
                          v.reshape(bsz, t_len, MOBA_DIM)).reshape(bsz * t_len, MOBA_DIM)
        h = _mix_prompt(u, vn, cq, gate, yb, h, mk.reshape(bsz, mem_len, MEM_DIM),
                        mv.reshape(bsz, mem_len, MEM_DIM), sgu_w[l], sgu_b[l].T,
                        lw["wa"], lw["wb"], lw["wc"], lw["wo"], tm=tm_p, t_len=t_len)
        h = _ffn(h, *lw["f2"], fin, tm=tm_p, final_norm=(l == depth - 1))
        kp.append(k.reshape(bsz, t_len, MOBA_HEADS, MOBA_HEAD_DIM))
        vp.append(v.reshape(bsz, t_len, MOBA_HEADS, MOBA_HEAD_DIM))
        mkp.append(mk.reshape(bsz, mem_len, MEM_HEADS, MEM_HEAD_DIM))
        mvp.append(mv.reshape(bsz, mem_len, MEM_HEADS, MEM_HEAD_DIM))
    y_prompt = h.reshape(bsz, t_len, D_MODEL)

    m_s = dec_b * SAMPLE_ROWS
    pos_s = past_len + jnp.arange(SAMPLE_ROWS, dtype=jnp.int32)
    cos_s, slo_s, shi_s = (jnp.tile(t, (dec_b, 1)) for t in _rope_tables(pos_s))
    h = jnp.pad(x_sample, ((0, 0), (0, SAMPLE_ROWS - t_new), (0, 0))).reshape(m_s, D_MODEL)
    n_b = 8
    ks_, vs_, sv_ = [], [], []
    unpad = lambda a: a.reshape(dec_b, SAMPLE_ROWS, -1)[:, :t_new]
    for l in range(depth):
        lw = layer_w[l]
        wd, bl = _sample_sgu_tables(sgu_w[l], sgu_b[l], n_b, t_new)
        h = _ffn(h, *lw["f1"], fin, tm=m_s, final_norm=False)
        u, vn, q, k, v, cq, gate = _inproj(h, row(mix_norm[l]), lw["win"], row(sgu_ln_g[l]),
                                           row(sgu_ln_b[l]), cos_s, slo_s, shi_s, tm=m_s)
        ya, yc = _premix_sample(u, vn, cq, cache_mem_k[l].reshape(dec_b, mem_len, MEM_DIM),
                                cache_mem_v[l].reshape(dec_b, mem_len, MEM_DIM), wd, bl, n_b=n_b)
        yb = _moba_sample(page_table, q, k, v, cache_k, cache_v, layer=l, n_q=t_new)
        h = _merge_sample(h, ya, yb, yc, gate, lw["wa"], lw["wb"], lw["wc"], lw["wo"])
        h = _ffn(h, *lw["f2"], fin, tm=m_s, final_norm=(l == depth - 1))
        ks_.append(unpad(k).reshape(dec_b, t_new, MOBA_HEADS, MOBA_HEAD_DIM))
        vs_.append(unpad(v).reshape(dec_b, t_new, MOBA_HEADS, MOBA_HEAD_DIM))
        sv_.append(unpad(vn))
    y_sample = unpad(h)

    return (y_prompt, y_sample, jnp.stack(kp), jnp.stack(vp), jnp.stack(mkp), jnp.stack(mvp),
            jnp.stack(ks_), jnp.stack(vs_), jnp.stack(sv_))
```

```python
import functools

import jax
import jax.numpy as jnp
import numpy as np
from jax import lax
from jax.experimental import pallas as pl
from jax.experimental.pallas import tpu as pltpu

F32 = jnp.float32
BF16 = jnp.bfloat16

D_MODEL = 1024
SGU_GROUPS = 4
SGU_GROUP_DIM = 128
SGU_DIM = 512
CHUNK = 128
MOBA_HEADS = 8
MOBA_HEAD_DIM = 64
MOBA_DIM = 512
MOBA_BLOCK = 256
MOBA_TOPK = 3
MEM_HEADS = 4
MEM_HEAD_DIM = 128
MEM_DIM = 512
D_FF = 2816
ROPE_THETA = 10000.0
NORM_EPS = 1e-6
MASK_VALUE = -1e30
IN_COLS = 6144
GATE_COL0 = 3072

V7X_VMEM_BYTES = 64 * 1024 * 1024
LANES = 128
SUBLANES = 8
SAMPLE_ROWS = 8
TEMP_ALLOWANCE = 12 * 1024 * 1024

NT_DIMS = (((1,), (1,)), ((), ()))


def _vmem_limit(block_bytes):
    return int(min(block_bytes + TEMP_ALLOWANCE, V7X_VMEM_BYTES - 4 * 1024 * 1024))


def _nbytes(shape, dtype):
    return int(np.prod(shape)) * jnp.dtype(dtype).itemsize


def _resident(shape):
    nd = len(shape)
    return pl.BlockSpec(shape, lambda *_: (0,) * nd, pipeline_mode=pl.Buffered(1))


def _dot(a, b):
    return jnp.dot(a, b, preferred_element_type=F32)


def _dot_nt(a, b, precision=None):
    return lax.dot_general(a, b, NT_DIMS, precision=precision, preferred_element_type=F32)


def _rms(x, g):
    return x * lax.rsqrt(jnp.mean(x * x, axis=-1, keepdims=True) + NORM_EPS) * g


def _sigmoid(x):
    return 1.0 / (1.0 + jnp.exp(-x))


def _gelu(x):
    return 0.5 * x * (1.0 + lax.erf(x * np.float32(np.sqrt(0.5))))


def _softmax_rows(s):
    p = jnp.exp(s - jnp.max(s, axis=-1, keepdims=True))
    return p, 1.0 / jnp.sum(p, axis=-1, keepdims=True)


_FF_CHUNK = 512
_FF_CHUNKS = tuple((s, min(_FF_CHUNK, D_FF - s)) for s in range(0, D_FF, _FF_CHUNK))


def _ffn_kernel(x_ref, g_ref, w1_ref, w3_ref, w2_ref, fg_ref, o_ref, a_ref, *, final_norm):
    x = x_ref[...]
    xb = _rms(x, g_ref[...]).astype(BF16)
    for s, w in _FF_CHUNKS:
        h1 = _dot(xb, w1_ref[:, s:s + w])
        h3 = _dot(xb, w3_ref[:, s:s + w])
        a_ref[:, s:s + w] = (h1 * _sigmoid(h1) * h3).astype(BF16)
    y = x + 0.5 * _dot(a_ref[...], w2_ref[...])
    if final_norm:
        y = _rms(y, fg_ref[...])
    o_ref[...] = y


def _ffn(x, g, w1, w3, w2, fg, *, tm, final_norm):
    m = x.shape[0]
    blocks = (2 * 2 * _nbytes((tm, D_MODEL), F32) + 3 * _nbytes((D_MODEL, D_FF), BF16)
              + _nbytes((tm, D_FF), BF16))
    return pl.pallas_call(
        functools.partial(_ffn_kernel, final_norm=final_norm),
        grid=(m // tm,),
        in_specs=[
            pl.BlockSpec((tm, D_MODEL), lambda i: (i, 0)),
            _resident((1, D_MODEL)),
            _resident((D_MODEL, D_FF)),
            _resident((D_MODEL, D_FF)),
            _resident((D_FF, D_MODEL)),
            _resident((1, D_MODEL)),
        ],
        out_specs=pl.BlockSpec((tm, D_MODEL), lambda i: (i, 0)),
        out_shape=jax.ShapeDtypeStruct((m, D_MODEL), F32),
        scratch_shapes=[pltpu.VMEM((tm, D_FF), BF16)],
        compiler_params=pltpu.CompilerParams(
            dimension_semantics=("parallel",), vmem_limit_bytes=_vmem_limit(blocks)),
        name="ffn",
    )(x, g, w1, w3, w2, fg)


def _rope(h, cos, sin_lo, sin_hi):
    outs = []
    for c in range(h.shape[1] // LANES):
        s = h[:, c * LANES:(c + 1) * LANES]
        outs.append(s * cos + pltpu.roll(s, LANES - 32, 1) * sin_lo + pltpu.roll(s, 32, 1) * sin_hi)
    return jnp.concatenate(outs, axis=1)


def _inproj_kernel(x_ref, g_ref, w_ref, lng_ref, lnb_ref, cos_ref, slo_ref, shi_ref,
                   u_ref, vn_ref, q_ref, k_ref, v_ref, cq_ref, gate_ref):
    xb = _rms(x_ref[...], g_ref[...]).astype(BF16)

    def proj(c0, w=512):
        return _dot(xb, w_ref[:, c0:c0 + w])

    u_ref[...] = _gelu(proj(0))
    a = _gelu(proj(512))
    ac = a - jnp.mean(a, axis=-1, keepdims=True)
    vn_ref[...] = (ac * lax.rsqrt(jnp.mean(ac * ac, axis=-1, keepdims=True) + NORM_EPS)
                   * lng_ref[...] + lnb_ref[...])
    cos, slo, shi = cos_ref[...], slo_ref[...], shi_ref[...]
    q_ref[...] = _rope(proj(1024), cos, slo, shi)
    k_ref[...] = _rope(proj(1536), cos, slo, shi)
    v_ref[...] = proj(2048)
    cq_ref[...] = proj(2560)
    for c in range(6):
        gate_ref[:, c * 512:(c + 1) * 512] = _sigmoid(proj(GATE_COL0 + c * 512))


def _inproj(x, g, w, lng, lnb, cos, slo, shi, *, tm):
    m = x.shape[0]
    n_tab = cos.shape[0] // tm
    out_cols = (512,) * 6 + (3 * D_MODEL,)
    blocks = (2 * _nbytes((tm, D_MODEL), F32) + _nbytes((D_MODEL, IN_COLS), BF16)
              + 2 * _nbytes((tm, IN_COLS), F32) + 6 * _nbytes((tm, LANES), F32))
    tab = pl.BlockSpec((tm, LANES), lambda i: (i % n_tab, 0))
    return pl.pallas_call(
        _inproj_kernel,
        grid=(m // tm,),
        in_specs=[
            pl.BlockSpec((tm, D_MODEL), lambda i: (i, 0)),
            _resident((1, D_MODEL)),
            _resident((D_MODEL, IN_COLS)),
            _resident((1, SGU_DIM)),
            _resident((1, SGU_DIM)),
            tab, tab, tab,
        ],
        out_specs=[pl.BlockSpec((tm, c), lambda i: (i, 0)) for c in out_cols],
        out_shape=[jax.ShapeDtypeStruct((m, c), F32) for c in out_cols],
        compiler_params=pltpu.CompilerParams(
            dimension_semantics=("parallel",), vmem_limit_bytes=_vmem_limit(blocks)),
        name="inproj",
    )(x, g, w, lng, lnb, cos, slo, shi)


def _rope_tables(pos):
    half = MOBA_HEAD_DIM // 2
    inv = ROPE_THETA ** (-jnp.arange(half, dtype=F32) * 2.0 / MOBA_HEAD_DIM)
    ang = pos.astype(F32)[:, None] * inv[None, :]
    cos, sin = jnp.tile(jnp.cos(ang), (1, 4)), jnp.tile(jnp.sin(ang), (1, 4))
    first_half = (jnp.arange(LANES) % MOBA_HEAD_DIM) < half
    return cos, jnp.where(first_half, -sin, 0.0), jnp.where(first_half, 0.0, sin)


def _memkv_kernel(m_ref, g_ref, w_ref, k_ref, v_ref):
    xb = _rms(m_ref[...], g_ref[...]).astype(BF16)
    k_ref[...] = _dot(xb, w_ref[:, :MEM_DIM])
    v_ref[...] = _dot(xb, w_ref[:, MEM_DIM:])


def _memkv(mem, g, w, *, tm):
    m = mem.shape[0]
    blocks = (2 * _nbytes((tm, D_MODEL), F32) + _nbytes((D_MODEL, 2 * MEM_DIM), BF16)
              + 4 * _nbytes((tm, MEM_DIM), F32))
    return pl.pallas_call(
        _memkv_kernel,
        grid=(m // tm,),
        in_specs=[pl.BlockSpec((tm, D_MODEL), lambda i: (i, 0)), _resident((1, D_MODEL)),
                  _resident((D_MODEL, 2 * MEM_DIM))],
        out_specs=[pl.BlockSpec((tm, MEM_DIM), lambda i: (i, 0))] * 2,
        out_shape=[jax.ShapeDtypeStruct((m, MEM_DIM), F32)] * 2,
        compiler_params=pltpu.CompilerParams(
            dimension_semantics=("parallel",), vmem_limit_bytes=_vmem_limit(blocks)),
        name="memkv",
    )(mem, g, w)


def _topk_select(g, n_past, n_rows):
    row = lax.broadcasted_iota(jnp.int32, g.shape, 0)
    rank = jnp.zeros(g.shape, F32)
    for m in range(n_past):
        gm = g[m:m + 1, :]
        beats = (gm > g) | ((gm == g) & (row > m))
        rank = rank + beats.astype(F32)
    return (row < n_past) & (rank < MOBA_TOPK)


def _moba_prompt_kernel(q_ref, k_ref, v_ref, o_ref):
    t_len = q_ref.shape[1]
    n_blk = t_len // MOBA_BLOCK
    k = k_ref[0]
    kb = k.astype(BF16)
    vb = v_ref[0].astype(BF16)
    k_mean = jnp.sum(k.reshape(n_blk, MOBA_BLOCK, LANES), axis=1) * (1.0 / MOBA_BLOCK)
    lane = lax.broadcasted_iota(jnp.int32, (1, LANES), 1)
    head_masks = (lane < MOBA_HEAD_DIM, lane >= MOBA_HEAD_DIM)
    r_i = lax.broadcasted_iota(jnp.int32, (2 * MOBA_BLOCK, MOBA_BLOCK), 0) & (MOBA_BLOCK - 1)
    c_i = lax.broadcasted_iota(jnp.int32, (2 * MOBA_BLOCK, MOBA_BLOCK), 1)
    causal2 = c_i <= r_i
    scale = MOBA_HEAD_DIM ** -0.5

    for qi in range(n_blk):
        q_blk = q_ref[0, qi * MOBA_BLOCK:(qi + 1) * MOBA_BLOCK, :]
        t_k = (qi + 1) * MOBA_BLOCK
        q2 = jnp.concatenate([jnp.where(hm, q_blk, 0.0) for hm in head_masks], axis=0)
        s = _dot_nt((q2 * scale).astype(BF16), kb[:t_k])
        parts = []
        if qi > MOBA_TOPK:
            gate = _dot_nt(k_mean, q2, precision=lax.Precision.HIGHEST)
            sel = _topk_select(gate, qi, n_blk).astype(F32)
            sel_t = jnp.concatenate(
                [sel, jnp.zeros((LANES - n_blk, 2 * MOBA_BLOCK), F32)], axis=0).T
            for n in range(qi):
                keep = sel_t[:, n:n + 1] > 0.5
                parts.append(jnp.where(keep, s[:, n * MOBA_BLOCK:(n + 1) * MOBA_BLOCK], MASK_VALUE))
        else:
            for n in range(qi):
                parts.append(s[:, n * MOBA_BLOCK:(n + 1) * MOBA_BLOCK])
        parts.append(jnp.where(causal2, s[:, qi * MOBA_BLOCK:t_k], MASK_VALUE))
        sm = jnp.concatenate(parts, axis=1) if len(parts) > 1 else parts[0]
        p, inv_l = _softmax_rows(sm)
        o2 = _dot(p.astype(BF16), vb[:t_k]) * inv_l
        o_ref[0, qi * MOBA_BLOCK:(qi + 1) * MOBA_BLOCK, :] = jnp.where(
            head_masks[0], o2[:MOBA_BLOCK], o2[MOBA_BLOCK:])


def _moba_prompt(q, k, v):
    bsz, t_len, _ = q.shape
    spec = pl.BlockSpec((1, t_len, LANES), lambda b, h: (b, 0, h))
    blocks = (8 * _nbytes((t_len, LANES), F32) + 2 * _nbytes((t_len, LANES), BF16)
              + 8 * _nbytes((MOBA_BLOCK, t_len), F32))
    return pl.pallas_call(
        _moba_prompt_kernel,
        grid=(bsz, MOBA_DIM // LANES),
        in_specs=[spec, spec, spec],
        out_specs=spec,
        out_shape=jax.ShapeDtypeStruct((bsz, t_len, MOBA_DIM), F32),
        compiler_params=pltpu.CompilerParams(
            dimension_semantics=("parallel", "parallel"), vmem_limit_bytes=_vmem_limit(blocks)),
        name="moba_prompt",
    )(q, k, v)


def _merge(x, ya, yb, yc, gate_ref, wa_ref, wb_ref, wc_ref, wo_ref):
    t = (gate_ref[:, 0:D_MODEL] * _dot(ya.astype(BF16), wa_ref[...])
         + gate_ref[:, D_MODEL:2 * D_MODEL] * _dot(yb.astype(BF16), wb_ref[...])
         + gate_ref[:, 2 * D_MODEL:3 * D_MODEL] * _dot(yc.astype(BF16), wc_ref[...]))
    return x + _dot(t.astype(BF16), wo_ref[...])


def _mem_attention(cq, mkb, mvb):
    scale = MEM_HEAD_DIM ** -0.5
    outs = []
    for h in range(MEM_HEADS):
        sl = slice(h * MEM_HEAD_DIM, (h + 1) * MEM_HEAD_DIM)
        s = _dot_nt(cq[:, sl].astype(BF16), mkb[:, sl]) * scale
        p, inv_l = _softmax_rows(s)
        outs.append(_dot(p.astype(BF16), mvb[:, sl]) * inv_l)
    return jnp.concatenate(outs, axis=1)


def _mix_prompt_kernel(u_ref, vn_ref, cq_ref, gate_ref, yb_ref, x_ref, mk_ref, mv_ref,
                       sw_ref, sb_ref, wa_ref, wb_ref, wc_ref, wo_ref, o_ref):
    tm = u_ref.shape[0]
    r_i = lax.broadcasted_iota(jnp.int32, (CHUNK, CHUNK), 0)
    c_i = lax.broadcasted_iota(jnp.int32, (CHUNK, CHUNK), 1)
    wm = [jnp.where(c_i <= r_i, sw_ref[g], 0.0).astype(BF16) for g in range(SGU_GROUPS)]
    rows = []
    for ch in range(tm // CHUNK):
        rs = slice(ch * CHUNK, (ch + 1) * CHUNK)
        cols = []
        for g in range(SGU_GROUPS):
            cs = slice(g * SGU_GROUP_DIM, (g + 1) * SGU_GROUP_DIM)
            y = _dot(wm[g], vn_ref[rs, cs].astype(BF16)) + sb_ref[:, g:g + 1]
            cols.append(u_ref[rs, cs] * y)
        rows.append(jnp.concatenate(cols, axis=1))
    ya = jnp.concatenate(rows, axis=0)
    yc = _mem_attention(cq_ref[...], mk_ref[0].astype(BF16), mv_ref[0].astype(BF16))
    o_ref[...] = _merge(x_ref[...], ya, yb_ref[...], yc, gate_ref, wa_ref, wb_ref, wc_ref, wo_ref)


def _mix_prompt(u, vn, cq, gate, yb, x, mk, mv, sw, sb_t, wa, wb, wc, wo, *, tm, t_len):
    m = x.shape[0]
    per_b = t_len // tm

    def rows(c):
        return pl.BlockSpec((tm, c), lambda i: (i, 0))

    mem_spec = pl.BlockSpec((1,) + mk.shape[1:], lambda i: (i // per_b, 0, 0))
    blocks = (2 * _nbytes((tm, 4 * 512 + 3 * D_MODEL + 2 * D_MODEL), F32)
              + 4 * _nbytes(mk.shape[1:], F32) + _nbytes((3 * 512 + D_MODEL, D_MODEL), BF16))
    return pl.pallas_call(
        _mix_prompt_kernel,
        grid=(m // tm,),
        in_specs=[rows(512), rows(512), rows(512), rows(3 * D_MODEL), rows(512), rows(D_MODEL),
                  mem_spec, mem_spec,
                  _resident(sw.shape), _resident(sb_t.shape),
                  _resident(wa.shape), _resident(wb.shape), _resident(wc.shape), _resident(wo.shape)],
        out_specs=rows(D_MODEL),
        out_shape=jax.ShapeDtypeStruct((m, D_MODEL), F32),
        compiler_params=pltpu.CompilerParams(
            dimension_semantics=("parallel",), vmem_limit_bytes=_vmem_limit(blocks)),
        name="mix_prompt",
    )(u, vn, cq, gate, yb, x, mk, mv, sw, sb_t, wa, wb, wc, wo)


def _premix_sample_kernel(u_ref, vn_ref, cq_ref, mk_ref, mv_ref, wd_ref, bl_ref, ya_ref, yc_ref):
    vn = vn_ref[...]
    y = bl_ref[...] + wd_ref[0] * vn
    for d in range(1, wd_ref.shape[0]):
        y = y + wd_ref[d] * pltpu.roll(vn, d, 0)
    ya_ref[...] = u_ref[...] * y

    lane = lax.broadcasted_iota(jnp.int32, (1, MEM_DIM), 1) // MEM_HEAD_DIM
    scale = MEM_HEAD_DIM ** -0.5
    for b in range(mk_ref.shape[0]):
        rs = slice(b * SAMPLE_ROWS, (b + 1) * SAMPLE_ROWS)
        q8 = cq_ref[rs, :]
        q_exp = jnp.concatenate([jnp.where(lane == h, q8, 0.0) for h in range(MEM_HEADS)], axis=0)
        s = _dot_nt(q_exp.astype(BF16), mk_ref[b].astype(BF16)) * scale
        p, inv_l = _softmax_rows(s)
        o = _dot(p.astype(BF16), mv_ref[b].astype(BF16)) * inv_l
        yc = jnp.zeros((SAMPLE_ROWS, MEM_DIM), F32)
        for h in range(MEM_HEADS):
            yc = yc + jnp.where(lane == h, o[h * SAMPLE_ROWS:(h + 1) * SAMPLE_ROWS, :], 0.0)
        yc_ref[rs, :] = yc


def _premix_sample(u, vn, cq, mk, mv, wd, bl, *, n_b):
    m = u.shape[0]
    tm = n_b * SAMPLE_ROWS
    rows = pl.BlockSpec((tm, 512), lambda i: (i, 0))
    mem_spec = pl.BlockSpec((n_b,) + mk.shape[1:], lambda i: (i, 0, 0))
    blocks = (2 * 5 * _nbytes((tm, 512), F32) + 4 * _nbytes((n_b,) + mk.shape[1:], F32)
              + _nbytes(wd.shape, F32) + _nbytes(bl.shape, F32))
    return pl.pallas_call(
        _premix_sample_kernel,
        grid=(m // tm,),
        in_specs=[rows, rows, rows, mem_spec, mem_spec, _resident(wd.shape), _resident(bl.shape)],
        out_specs=[rows, rows],
        out_shape=[jax.ShapeDtypeStruct((m, 512), F32)] * 2,
        compiler_params=pltpu.CompilerParams(
            dimension_semantics=("parallel",), vmem_limit_bytes=_vmem_limit(blocks)),
        name="premix_sample",
    )(u, vn, cq, mk, mv, wd, bl)


def _merge_kernel(x_ref, ya_ref, yb_ref, yc_ref, gate_ref, wa_ref, wb_ref, wc_ref, wo_ref, o_ref):
    o_ref[...] = _merge(x_ref[...], ya_ref[...], yb_ref[...], yc_ref[...], gate_ref,
                        wa_ref, wb_ref, wc_ref, wo_ref)


def _merge_sample(x, ya, yb, yc, gate, wa, wb, wc, wo):
    m = x.shape[0]
    args = (x, ya, yb, yc, gate, wa, wb, wc, wo)
    blocks = sum(_nbytes(a.shape, a.dtype) for a in args) + _nbytes(x.shape, F32)
    return pl.pallas_call(
        _merge_kernel,
        grid=(1,),
        in_specs=[_resident(a.shape) for a in args],
        out_specs=pl.BlockSpec((m, D_MODEL), lambda i: (0, 0)),
        out_shape=jax.ShapeDtypeStruct((m, D_MODEL), F32),
        compiler_params=pltpu.CompilerParams(
            dimension_semantics=("arbitrary",), vmem_limit_bytes=_vmem_limit(blocks)),
        name="merge_sample",
    )(*args)


_PAGES_PER_CHUNK = 8
_RING_SLOTS = 4
_RING_AHEAD = _RING_SLOTS - 1


def _moba_sample_kernel(pt_ref, q_ref, kn_ref, vnew_ref, ck_hbm, cv_hbm, o_ref,
                        buf, sem, s_ref, *, page_base, n_pages, page_size, n_q):
    b = pl.program_id(0)
    n_req = pl.num_programs(0)
    chunk_keys = _PAGES_PER_CHUNK * page_size
    n_chunks = n_pages // _PAGES_PER_CHUNK
    blocks_per_chunk = chunk_keys // MOBA_BLOCK
    n_past_blk = n_chunks * blocks_per_chunk
    past_len = n_pages * page_size
    q_rows = n_q * MOBA_HEADS

    def copies(req, c, slot):
        src = ck_hbm if c < n_chunks else cv_hbm
        first = req * n_pages + (c % n_chunks) * _PAGES_PER_CHUNK
        return [pltpu.make_async_copy(src.at[pt_ref[first + p] + page_base],
                                      buf.at[slot, :, pl.ds(p * page_size, page_size)],
                                      sem.at[slot])
                for p in range(_PAGES_PER_CHUNK)]

    def start(req, c, slot):
        for cp in copies(req, c, slot):
            cp.start()

    def wait(req, c, slot):
        for cp in copies(req, c, slot):
            cp.wait()

    n_total = 2 * n_chunks

    @pl.when(b == 0)
    def _():
        for c0 in range(_RING_AHEAD):
            start(b, c0, c0 % _RING_SLOTS)

    sub_i = lax.broadcasted_iota(jnp.int32, (SAMPLE_ROWS, MOBA_DIM), 0)
    own_head = sub_i == lax.broadcasted_iota(jnp.int32, (SAMPLE_ROWS, MOBA_DIM), 1) // MOBA_HEAD_DIM
    q8 = q_ref[...] * (MOBA_HEAD_DIM ** -0.5)
    q_exp = jnp.concatenate(
        [jnp.where(own_head, jnp.broadcast_to(q8[t:t + 1, :], (MOBA_HEADS, MOBA_DIM)), 0.0)
         for t in range(n_q)], axis=0)
    q_exp_b = q_exp.astype(BF16)

    def pad_rows(x8):
        return jnp.concatenate([x8, jnp.zeros((LANES - SAMPLE_ROWS, x8.shape[1]), F32)], axis=0)

    blk_lane = lax.broadcasted_iota(jnp.int32, (MOBA_DIM, LANES), 1)
    ksum = jnp.zeros((MOBA_DIM, LANES), F32)
    acc = jnp.zeros((q_rows, MOBA_DIM), F32)
    inv_l = None
    for c in range(n_total):
        slot = c % _RING_SLOTS
        wait(b, c, slot)
        nxt = c + _RING_AHEAD
        if nxt < n_total:
            start(b, nxt, nxt % _RING_SLOTS)
        else:
            @pl.when(b + 1 < n_req)
            def _(nxt=nxt):
                start(b + 1, nxt - n_total, nxt % _RING_SLOTS)

        if c < n_chunks:
            kc = buf[slot]
            for j in range(blocks_per_chunk):
                col = jnp.sum(kc[:, j * MOBA_BLOCK:(j + 1) * MOBA_BLOCK], axis=1, keepdims=True)
                ksum = jnp.where(blk_lane == c * blocks_per_chunk + j, col, ksum)
            s_ref[:, c * chunk_keys:(c + 1) * chunk_keys] = _dot(q_exp_b, kc.astype(BF16))
            if c == n_chunks - 1:
                gate = jnp.dot(q_exp, ksum * (1.0 / MOBA_BLOCK), precision=lax.Precision.HIGHEST,
                               preferred_element_type=F32)
                lane = lax.broadcasted_iota(jnp.int32, gate.shape, 1)
                rank = jnp.zeros(gate.shape, F32)
                for m in range(n_past_blk):
                    gm = gate[:, m:m + 1]
                    beats = (gm > gate) | ((gm == gate) & (lane > m))
                    rank = rank + beats.astype(F32)
                sel = ((lane < n_past_blk) & (rank < MOBA_TOPK)).astype(F32)
                s_own = _dot_nt(q_exp_b, pad_rows(kn_ref[...]).astype(BF16))
                t_i = lax.broadcasted_iota(jnp.int32, s_own.shape, 0) // MOBA_HEADS
                s_own = jnp.where(lane <= t_i, s_own, MASK_VALUE)
                mx_acc = jnp.full((q_rows, MOBA_BLOCK), MASK_VALUE, F32)
                for n in range(n_past_blk):
                    cs = slice(n * MOBA_BLOCK, (n + 1) * MOBA_BLOCK)
                    blk = jnp.where(sel[:, n:n + 1] > 0.5, s_ref[:, cs], MASK_VALUE)
                    s_ref[:, cs] = blk
                    mx_acc = jnp.maximum(mx_acc, blk)
                mx = jnp.maximum(jnp.max(mx_acc, axis=1, keepdims=True),
                                 jnp.max(s_own, axis=1, keepdims=True))
                p_own = jnp.exp(s_own - mx)
                l_acc = jnp.zeros((q_rows, MOBA_BLOCK), F32)
                for n in range(n_past_blk):
                    cs = slice(n * MOBA_BLOCK, (n + 1) * MOBA_BLOCK)
                    blk = jnp.exp(s_ref[:, cs] - mx)
                    s_ref[:, cs] = blk
                    l_acc = l_acc + blk
                s_ref[:, past_len:past_len + LANES] = p_own
                inv_l = 1.0 / (jnp.sum(l_acc, axis=1, keepdims=True)
                               + jnp.sum(p_own, axis=1, keepdims=True))
        else:
            cc = c - n_chunks
            p_c = s_ref[:, cc * chunk_keys:(cc + 1) * chunk_keys].astype(BF16)
            acc = acc + _dot_nt(p_c, buf[slot].astype(BF16))

    p_own = s_ref[:, past_len:past_len + LANES].astype(BF16)
    acc = (acc + _dot(p_own, pad_rows(vnew_ref[...]).astype(BF16))) * inv_l
    y = jnp.zeros((SAMPLE_ROWS, MOBA_DIM), F32)
    for t in range(n_q):
        grp = jnp.where(own_head, acc[t * MOBA_HEADS:(t + 1) * MOBA_HEADS, :], 0.0)
        y = jnp.where(sub_i == t, jnp.sum(grp, axis=0, keepdims=True), y)
    o_ref[...] = y


def _moba_sample(page_table, q, k_new, v_new, cache_k, cache_v, *, layer, n_q):
    n_req, n_pages = page_table.shape
    depth, n_pool, page_size = cache_k.shape[:3]
    ck = jnp.transpose(cache_k, (0, 1, 3, 4, 2)).reshape(depth * n_pool, MOBA_DIM, page_size)
    cv = jnp.transpose(cache_v, (0, 1, 3, 4, 2)).reshape(depth * n_pool, MOBA_DIM, page_size)
    chunk_keys = _PAGES_PER_CHUNK * page_size
    past_len = n_pages * page_size
    q_rows = n_q * MOBA_HEADS
    rows = pl.BlockSpec((SAMPLE_ROWS, MOBA_DIM), lambda b, pt: (b, 0))
    hbm = pl.BlockSpec(memory_space=pl.ANY)
    assert (2 * n_pages // _PAGES_PER_CHUNK) % _RING_SLOTS == 0
    scratch = [pltpu.VMEM((_RING_SLOTS, MOBA_DIM, chunk_keys), F32),
               pltpu.SemaphoreType.DMA((_RING_SLOTS,)),
               pltpu.VMEM((q_rows, past_len + LANES), F32)]
    blocks = (_nbytes((_RING_SLOTS, MOBA_DIM, chunk_keys), F32) + _nbytes((q_rows, past_len + LANES), F32)
              + 8 * _nbytes((SAMPLE_ROWS, MOBA_DIM), F32))
    return pl.pallas_call(
        functools.partial(_moba_sample_kernel, page_base=layer * n_pool, n_pages=n_pages,
                          page_size=page_size, n_q=n_q),
        grid_spec=pltpu.PrefetchScalarGridSpec(
            num_scalar_prefetch=1,
            grid=(n_req,),
            in_specs=[rows, rows, rows, hbm, hbm],
            out_specs=rows,
            scratch_shapes=scratch),
        out_shape=jax.ShapeDtypeStruct((n_req * SAMPLE_ROWS, MOBA_DIM), F32),
        compiler_params=pltpu.CompilerParams(
            dimension_semantics=("arbitrary",), vmem_limit_bytes=_vmem_limit(blocks)),
        name="moba_sample",
    )(page_table.reshape(-1), q, k_new, v_new, ck, cv)


def _sample_sgu_tables(sgu_w, sgu_b, n_b, t_new):
    t = jnp.arange(SAMPLE_ROWS)
    d = jnp.arange(t_new)
    src = t[None, :] - d[:, None]
    ok = (src >= 0) & (t[None, :] < t_new)
    w_td = sgu_w[:, t[None, :].clip(0, t_new - 1), src.clip(0, t_new - 1)]
    w_td = jnp.where(ok[None], w_td, 0.0)
    wd = jnp.repeat(jnp.transpose(w_td, (1, 2, 0)), SGU_GROUP_DIM, axis=2)
    bl = jnp.where((t < t_new)[:, None],
                   jnp.repeat(sgu_b[:, t.clip(0, t_new - 1)].T, SGU_GROUP_DIM, axis=1), 0.0)
    return jnp.tile(wd, (1, n_b, 1)), jnp.tile(bl, (n_b, 1))


def kernel(x_prompt, x_sample, mem_prompt, cache_k, cache_v, cache_mem_k, cache_mem_v, page_table,
           ffn1_norm, ffn1_w1, ffn1_w3, ffn1_w2, mix_norm, w_in, sgu_ln_g, sgu_ln_b, sgu_w, sgu_b,
           mem_norm, w_mem_kv, w_a_out, w_b_out, w_c_out, w_o, ffn2_norm, ffn2_w1, ffn2_w3, ffn2_w2,
           final_norm):
    depth = w_in.shape[0]
    bsz, t_len, _ = x_prompt.shape
    dec_b, t_new, _ = x_sample.shape
    n_pages = page_table.shape[1]
    page_size = cache_k.shape[2]
    past_len = n_pages * page_size
    mem_len = mem_prompt.shape[1]
    assert t_len % 512 == 0 and t_new <= SAMPLE_ROWS and past_len % MOBA_BLOCK == 0
    assert n_pages % _PAGES_PER_CHUNK == 0 and (_PAGES_PER_CHUNK * page_size) % MOBA_BLOCK == 0

    row = lambda a: a.reshape(1, -1)
    bf = lambda a: a.astype(BF16)
    fin = row(final_norm)

    tm_p = 512
    cos_p, slo_p, shi_p = _rope_tables(jnp.arange(t_len, dtype=jnp.int32))
    h = x_prompt.reshape(bsz * t_len, D_MODEL)
    mem = mem_prompt.reshape(bsz * mem_len, D_MODEL)
    kp, vp, mkp, mvp = [], [], [], []
    layer_w = []
    for l in range(depth):
        layer_w.append(dict(
            f1=(row(ffn1_norm[l]), bf(ffn1_w1[l]), bf(ffn1_w3[l]), bf(ffn1_w2[l])),
            f2=(row(ffn2_norm[l]), bf(ffn2_w1[l]), bf(ffn2_w3[l]), bf(ffn2_w2[l])),
            win=bf(w_in[l]), wa=bf(w_a_out[l]), wb=bf(w_b_out[l]), wc=bf(w_c_out[l]), wo=bf(w_o[l])))
    for l in range(depth):
        lw = layer_w[l]
        mk, mv = _memkv(mem, row(mem_norm[l]), bf(w_mem_kv[l]), tm=512)
        h = _ffn(h, *lw["f1"], fin, tm=tm_p, final_norm=False)
        u, vn, q, k, v, cq, gate = _inproj(h, row(mix_norm[l]), lw["win"], row(sgu_ln_g[l]),
                                           row(sgu_ln_b[l]), cos_p, slo_p, shi_p, tm=tm_p)
        yb = _moba_prompt(q.reshape(bsz, t_len, MOBA_DIM), k.reshape(bsz, t_len, MOBA_DIM),
                          v.reshape(bsz, t_len, MOBA_DIM)).reshape(bsz * t_len, MOBA_DIM)
        h = _mix_prompt(u, vn, cq, gate, yb, h, mk.reshape(bsz, mem_len, MEM_DIM),
                        mv.reshape(bsz, mem_len, MEM_DIM), sgu_w[l], sgu_b[l].T,
                        lw["wa"], lw["wb"], lw["wc"], lw["wo"], tm=tm_p, t_len=t_len)
        h = _ffn(h, *lw["f2"], fin, tm=tm_p, final_norm=(l == depth - 1))
        kp.append(k.reshape(bsz, t_len, MOBA_HEADS, MOBA_HEAD_DIM))
        vp.append(v.reshape(bsz, t_len, MOBA_HEADS, MOBA_HEAD_DIM))
        mkp.append(mk.reshape(bsz, mem_len, MEM_HEADS, MEM_HEAD_DIM))
        mvp.append(mv.reshape(bsz, mem_len, MEM_HEADS, MEM_HEAD_DIM))
    y_prompt = h.reshape(bsz, t_len, D_MODEL)

    m_s = dec_b * SAMPLE_ROWS
    pos_s = past_len + jnp.arange(SAMPLE_ROWS, dtype=jnp.int32)
    cos_s, slo_s, shi_s = (jnp.tile(t, (dec_b, 1)) for t in _rope_tables(pos_s))
    h = jnp.pad(x_sample, ((0, 0), (0, SAMPLE_ROWS - t_new), (0, 0))).reshape(m_s, D_MODEL)
    n_b = 8
    ks_, vs_, sv_ = [], [], []
    unpad = lambda a: a.reshape(dec_b, SAMPLE_ROWS, -1)[:, :t_new]
    for l in range(depth):
        lw = layer_w[l]
        wd, bl = _sample_sgu_tables(sgu_w[l], sgu_b[l], n_b, t_new)
        h = _ffn(h, *lw["f1"], fin, tm=m_s, final_norm=False)
        u, vn, q, k, v, cq, gate = _inproj(h, row(mix_norm[l]), lw["win"], row(sgu_ln_g[l]),
                                           row(sgu_ln_b[l]), cos_s, slo_s, shi_s, tm=m_s)
        ya, yc = _premix_sample(u, vn, cq, cache_mem_k[l].reshape(dec_b, mem_len, MEM_DIM),
                                cache_mem_v[l].reshape(dec_b, mem_len, MEM_DIM), wd, bl, n_b=n_b)
        yb = _moba_sample(page_table, q, k, v, cache_k, cache_v, layer=l, n_q=t_new)
        h = _merge_sample(h, ya, yb, yc, gate, lw["wa"], lw["wb"], lw["wc"], lw["wo"])
        h = _ffn(h, *lw["f2"], fin, tm=m_s, final_norm=(l == depth - 1))
        ks_.append(unpad(k).reshape(dec_b, t_new, MOBA_HEADS, MOBA_HEAD_DIM))
        vs_.append(unpad(v).reshape(dec_b, t_new, MOBA_HEADS, MOBA_HEAD_DIM))
        sv_.append(unpad(vn))
    y_sample = unpad(h)

    return (y_prompt, y_sample, jnp.stack(kp), jnp.stack(vp), jnp.stack(mkp), jnp.stack(mvp),
            jnp.stack(ks_), jnp.stack(vs_), jnp.stack(sv_))
```

```python
import functools

import jax
import jax.numpy as jnp
import numpy as np
from jax import lax
from jax.experimental import pallas as pl
from jax.experimental.pallas import tpu as pltpu

F32 = jnp.float32
BF16 = jnp.bfloat16

D_MODEL = 1024
SGU_GROUPS = 4
SGU_GROUP_DIM = 128
SGU_DIM = 512
CHUNK = 128
MOBA_HEADS = 8
MOBA_HEAD_DIM = 64
MOBA_DIM = 512
MOBA_BLOCK = 256
MOBA_TOPK = 3
MEM_HEADS = 4
MEM_HEAD_DIM = 128
MEM_DIM = 512
D_FF = 2816
ROPE_THETA = 10000.0
NORM_EPS = 1e-6
MASK_VALUE = -1e30
IN_COLS = 6144
GATE_COL0 = 3072

V7X_VMEM_BYTES = 64 * 1024 * 1024
LANES = 128
SUBLANES = 8
SAMPLE_ROWS = 8
TEMP_ALLOWANCE = 12 * 1024 * 1024

NT_DIMS = (((1,), (1,)), ((), ()))


def _vmem_limit(block_bytes):
    return int(min(block_bytes + TEMP_ALLOWANCE, V7X_VMEM_BYTES - 4 * 1024 * 1024))


def _nbytes(shape, dtype):
    return int(np.prod(shape)) * jnp.dtype(dtype).itemsize


def _resident(shape):
    nd = len(shape)
    return pl.BlockSpec(shape, lambda *_: (0,) * nd, pipeline_mode=pl.Buffered(1))


def _layer_resident(arr, layer):
    nd = arr.ndim
    return pl.BlockSpec((None,) + arr.shape[1:], lambda *_: (layer,) + (0,) * (nd - 1),
                        pipeline_mode=pl.Buffered(1))


def _dot(a, b):
    return jnp.dot(a, b, preferred_element_type=F32)


def _dot_nt(a, b, precision=None):
    return lax.dot_general(a, b, NT_DIMS, precision=precision, preferred_element_type=F32)


def _rms(x, g):
    return x * lax.rsqrt(jnp.mean(x * x, axis=-1, keepdims=True) + NORM_EPS) * g


def _sigmoid(x):
    return 1.0 / (1.0 + jnp.exp(-x))


def _gelu(x):
    return 0.5 * x * (1.0 + lax.erf(x * np.float32(np.sqrt(0.5))))


def _softmax_rows(s):
    p = jnp.exp(s - jnp.max(s, axis=-1, keepdims=True))
    return p, 1.0 / jnp.sum(p, axis=-1, keepdims=True)


_FF_CHUNK = 512
_FF_CHUNKS = tuple((s, min(_FF_CHUNK, D_FF - s)) for s in range(0, D_FF, _FF_CHUNK))


def _ffn_kernel(x_ref, g_ref, w1_ref, w3_ref, w2_ref, fg_ref, o_ref, a_ref, *, final_norm):
    x = x_ref[...]
    xb = _rms(x, g_ref[...]).astype(BF16)
    for s, w in _FF_CHUNKS:
        h1 = _dot(xb, w1_ref[:, s:s + w])
        h3 = _dot(xb, w3_ref[:, s:s + w])
        a_ref[:, s:s + w] = (h1 * _sigmoid(h1) * h3).astype(BF16)
    y = x + 0.5 * _dot(a_ref[...], w2_ref[...])
    if final_norm:
        y = _rms(y, fg_ref[...])
    o_ref[...] = y


def _ffn(x, g, w1, w3, w2, fg, *, layer, tm, final_norm):
    m = x.shape[0]
    blocks = (2 * 2 * _nbytes((tm, D_MODEL), F32) + 3 * _nbytes((D_MODEL, D_FF), BF16)
              + _nbytes((tm, D_FF), BF16))
    return pl.pallas_call(
        functools.partial(_ffn_kernel, final_norm=final_norm),
        grid=(m // tm,),
        in_specs=[
            pl.BlockSpec((tm, D_MODEL), lambda i: (i, 0)),
            _resident((1, D_MODEL)),
            _layer_resident(w1, layer),
            _layer_resident(w3, layer),
            _layer_resident(w2, layer),
            _resident((1, D_MODEL)),
        ],
        out_specs=pl.BlockSpec((tm, D_MODEL), lambda i: (i, 0)),
        out_shape=jax.ShapeDtypeStruct((m, D_MODEL), F32),
        scratch_shapes=[pltpu.VMEM((tm, D_FF), BF16)],
        compiler_params=pltpu.CompilerParams(
            dimension_semantics=("parallel",), vmem_limit_bytes=_vmem_limit(blocks)),
        name="ffn",
    )(x, g, w1, w3, w2, fg)


def _rope(h, cos, sin_lo, sin_hi):
    outs = []
    for c in range(h.shape[1] // LANES):
        s = h[:, c * LANES:(c + 1) * LANES]
        outs.append(s * cos + pltpu.roll(s, LANES - 32, 1) * sin_lo + pltpu.roll(s, 32, 1) * sin_hi)
    return jnp.concatenate(outs, axis=1)


def _inproj_kernel(x_ref, g_ref, w_ref, lng_ref, lnb_ref, cos_ref, slo_ref, shi_ref,
                   u_ref, vn_ref, q_ref, k_ref, v_ref, cq_ref, gate_ref, *t_refs):
    xb = _rms(x_ref[...], g_ref[...]).astype(BF16)

    def proj(c0, w=512):
        return _dot(xb, w_ref[:, c0:c0 + w])

    u_ref[...] = _gelu(proj(0))
    a = _gelu(proj(512))
    ac = a - jnp.mean(a, axis=-1, keepdims=True)
    vn_ref[...] = (ac * lax.rsqrt(jnp.mean(ac * ac, axis=-1, keepdims=True) + NORM_EPS)
                   * lng_ref[...] + lnb_ref[...])
    cos, slo, shi = cos_ref[...], slo_ref[...], shi_ref[...]
    q_ref[...] = _rope(proj(1024), cos, slo, shi)
    k = _rope(proj(1536), cos, slo, shi)
    v = proj(2048)
    k_ref[...] = k
    v_ref[...] = v
    if t_refs:
        t_refs[0][...] = k.T
        t_refs[1][...] = v.T
    cq_ref[...] = proj(2560)
    for c in range(6):
        gate_ref[:, c * 512:(c + 1) * 512] = _sigmoid(proj(GATE_COL0 + c * 512))


def _inproj(x, g, w, lng, lnb, cos, slo, shi, *, layer, tm, t_major_kv):
    m = x.shape[0]
    t_len = cos.shape[0]
    n_tab = t_len // tm
    out_cols = (512,) * 6 + (3 * D_MODEL,)
    blocks = (2 * _nbytes((tm, D_MODEL), F32) + _nbytes((D_MODEL, IN_COLS), BF16)
              + 2 * _nbytes((tm, IN_COLS), F32) + 6 * _nbytes((tm, LANES), F32))
    tab = pl.BlockSpec((tm, LANES), lambda i: (i % n_tab, 0))
    out_specs = [pl.BlockSpec((tm, c), lambda i: (i, 0)) for c in out_cols]
    out_shape = [jax.ShapeDtypeStruct((m, c), F32) for c in out_cols]
    if t_major_kv:
        blocks += 4 * _nbytes((MOBA_DIM, tm), F32)
        out_specs += [pl.BlockSpec((None, MOBA_DIM, tm), lambda i: (i // n_tab, 0, i % n_tab))] * 2
        out_shape += [jax.ShapeDtypeStruct((m // t_len, MOBA_DIM, t_len), F32)] * 2
    return pl.pallas_call(
        _inproj_kernel,
        grid=(m // tm,),
        in_specs=[
            pl.BlockSpec((tm, D_MODEL), lambda i: (i, 0)),
            _resident((1, D_MODEL)),
            _layer_resident(w, layer),
            _resident((1, SGU_DIM)),
            _resident((1, SGU_DIM)),
            tab, tab, tab,
        ],
        out_specs=out_specs,
        out_shape=out_shape,
        compiler_params=pltpu.CompilerParams(
            dimension_semantics=("parallel",), vmem_limit_bytes=_vmem_limit(blocks)),
        name="inproj",
    )(x, g, w, lng, lnb, cos, slo, shi)


def _rope_tables(pos):
    half = MOBA_HEAD_DIM // 2
    inv = ROPE_THETA ** (-jnp.arange(half, dtype=F32) * 2.0 / MOBA_HEAD_DIM)
    ang = pos.astype(F32)[:, None] * inv[None, :]
    cos, sin = jnp.tile(jnp.cos(ang), (1, 4)), jnp.tile(jnp.sin(ang), (1, 4))
    first_half = (jnp.arange(LANES) % MOBA_HEAD_DIM) < half
    return cos, jnp.where(first_half, -sin, 0.0), jnp.where(first_half, 0.0, sin)


def _memkv_kernel(m_ref, g_ref, w_ref, k_ref, v_ref):
    xb = _rms(m_ref[...], g_ref[...]).astype(BF16)
    k_ref[...] = _dot(xb, w_ref[:, :MEM_DIM])
    v_ref[...] = _dot(xb, w_ref[:, MEM_DIM:])


def _memkv(mem, g, w, *, layer, tm):
    m = mem.shape[0]
    blocks = (2 * _nbytes((tm, D_MODEL), F32) + _nbytes((D_MODEL, 2 * MEM_DIM), BF16)
              + 4 * _nbytes((tm, MEM_DIM), F32))
    return pl.pallas_call(
        _memkv_kernel,
        grid=(m // tm,),
        in_specs=[pl.BlockSpec((tm, D_MODEL), lambda i: (i, 0)), _resident((1, D_MODEL)),
                  _layer_resident(w, layer)],
        out_specs=[pl.BlockSpec((tm, MEM_DIM), lambda i: (i, 0))] * 2,
        out_shape=[jax.ShapeDtypeStruct((m, MEM_DIM), F32)] * 2,
        compiler_params=pltpu.CompilerParams(
            dimension_semantics=("parallel",), vmem_limit_bytes=_vmem_limit(blocks)),
        name="memkv",
    )(mem, g, w)


def _topk_select(g, n_past, n_rows):
    row = lax.broadcasted_iota(jnp.int32, g.shape, 0)
    rank = jnp.zeros(g.shape, F32)
    for m in range(n_past):
        gm = g[m:m + 1, :]
        beats = (gm > g) | ((gm == g) & (row > m))
        rank = rank + beats.astype(F32)
    return (row < n_past) & (rank < MOBA_TOPK)


def _moba_prompt_kernel(q_ref, k_ref, v_ref, o_ref):
    t_len = q_ref.shape[1]
    n_blk = t_len // MOBA_BLOCK
    k = k_ref[0]
    kb = k.astype(BF16)
    vb = v_ref[0].astype(BF16)
    k_mean = jnp.sum(k.reshape(n_blk, MOBA_BLOCK, LANES), axis=1) * (1.0 / MOBA_BLOCK)
    lane = lax.broadcasted_iota(jnp.int32, (1, LANES), 1)
    head_masks = (lane < MOBA_HEAD_DIM, lane >= MOBA_HEAD_DIM)
    r_i = lax.broadcasted_iota(jnp.int32, (2 * MOBA_BLOCK, MOBA_BLOCK), 0) & (MOBA_BLOCK - 1)
    c_i = lax.broadcasted_iota(jnp.int32, (2 * MOBA_BLOCK, MOBA_BLOCK), 1)
    causal2 = c_i <= r_i
    scale = MOBA_HEAD_DIM ** -0.5

    for qi in range(n_blk):
        q_blk = q_ref[0, qi * MOBA_BLOCK:(qi + 1) * MOBA_BLOCK, :]
        t_k = (qi + 1) * MOBA_BLOCK
        q2 = jnp.concatenate([jnp.where(hm, q_blk, 0.0) for hm in head_masks], axis=0)
        s = _dot_nt((q2 * scale).astype(BF16), kb[:t_k])
        parts = []
        if qi > MOBA_TOPK:
            gate = _dot_nt(k_mean, q2, precision=lax.Precision.HIGHEST)
            sel = _topk_select(gate, qi, n_blk).astype(F32)
            sel_t = jnp.concatenate(
                [sel, jnp.zeros((LANES - n_blk, 2 * MOBA_BLOCK), F32)], axis=0).T
            for n in range(qi):
                keep = sel_t[:, n:n + 1] > 0.5
                parts.append(jnp.where(keep, s[:, n * MOBA_BLOCK:(n + 1) * MOBA_BLOCK], MASK_VALUE))
        else:
            for n in range(qi):
                parts.append(s[:, n * MOBA_BLOCK:(n + 1) * MOBA_BLOCK])
        parts.append(jnp.where(causal2, s[:, qi * MOBA_BLOCK:t_k], MASK_VALUE))
        sm = jnp.concatenate(parts, axis=1) if len(parts) > 1 else parts[0]
        p, inv_l = _softmax_rows(sm)
        o2 = _dot(p.astype(BF16), vb[:t_k]) * inv_l
        o_ref[0, qi * MOBA_BLOCK:(qi + 1) * MOBA_BLOCK, :] = jnp.where(
            head_masks[0], o2[:MOBA_BLOCK], o2[MOBA_BLOCK:])


def _moba_prompt(q, k, v):
    bsz, t_len, _ = q.shape
    spec = pl.BlockSpec((1, t_len, LANES), lambda b, h: (b, 0, h))
    blocks = (8 * _nbytes((t_len, LANES), F32) + 2 * _nbytes((t_len, LANES), BF16)
              + 8 * _nbytes((MOBA_BLOCK, t_len), F32))
    return pl.pallas_call(
        _moba_prompt_kernel,
        grid=(bsz, MOBA_DIM // LANES),
        in_specs=[spec, spec, spec],
        out_specs=spec,
        out_shape=jax.ShapeDtypeStruct((bsz, t_len, MOBA_DIM), F32),
        compiler_params=pltpu.CompilerParams(
            dimension_semantics=("parallel", "parallel"), vmem_limit_bytes=_vmem_limit(blocks)),
        name="moba_prompt",
    )(q, k, v)


def _merge(x, ya, yb, yc, gate_ref, wa_ref, wb_ref, wc_ref, wo_ref):
    t = (gate_ref[:, 0:D_MODEL] * _dot(ya.astype(BF16), wa_ref[...])
         + gate_ref[:, D_MODEL:2 * D_MODEL] * _dot(yb.astype(BF16), wb_ref[...])
         + gate_ref[:, 2 * D_MODEL:3 * D_MODEL] * _dot(yc.astype(BF16), wc_ref[...]))
    return x + _dot(t.astype(BF16), wo_ref[...])


def _mem_attention(cq, mkb, mvb):
    scale = MEM_HEAD_DIM ** -0.5
    outs = []
    for h in range(MEM_HEADS):
        sl = slice(h * MEM_HEAD_DIM, (h + 1) * MEM_HEAD_DIM)
        s = _dot_nt(cq[:, sl].astype(BF16), mkb[:, sl]) * scale
        p, inv_l = _softmax_rows(s)
        outs.append(_dot(p.astype(BF16), mvb[:, sl]) * inv_l)
    return jnp.concatenate(outs, axis=1)


def _mix_prompt_kernel(u_ref, vn_ref, cq_ref, gate_ref, yb_ref, x_ref, mk_ref, mv_ref,
                       sw_ref, sb_ref, wa_ref, wb_ref, wc_ref, wo_ref, o_ref):
    tm = u_ref.shape[0]
    r_i = lax.broadcasted_iota(jnp.int32, (CHUNK, CHUNK), 0)
    c_i = lax.broadcasted_iota(jnp.int32, (CHUNK, CHUNK), 1)
    wm = [jnp.where(c_i <= r_i, sw_ref[g], 0.0).astype(BF16) for g in range(SGU_GROUPS)]
    rows = []
    for ch in range(tm // CHUNK):
        rs = slice(ch * CHUNK, (ch + 1) * CHUNK)
        cols = []
        for g in range(SGU_GROUPS):
            cs = slice(g * SGU_GROUP_DIM, (g + 1) * SGU_GROUP_DIM)
            y = _dot(wm[g], vn_ref[rs, cs].astype(BF16)) + sb_ref[:, g:g + 1]
            cols.append(u_ref[rs, cs] * y)
        rows.append(jnp.concatenate(cols, axis=1))
    ya = jnp.concatenate(rows, axis=0)
    yc = _mem_attention(cq_ref[...], mk_ref[0].astype(BF16), mv_ref[0].astype(BF16))
    o_ref[...] = _merge(x_ref[...], ya, yb_ref[...], yc, gate_ref, wa_ref, wb_ref, wc_ref, wo_ref)


def _mix_prompt(u, vn, cq, gate, yb, x, mk, mv, sw, sb_t, wa, wb, wc, wo, *, layer, tm, t_len):
    m = x.shape[0]
    per_b = t_len // tm

    def rows(c):
        return pl.BlockSpec((tm, c), lambda i: (i, 0))

    mem_spec = pl.BlockSpec((1,) + mk.shape[1:], lambda i: (i // per_b, 0, 0))
    blocks = (2 * _nbytes((tm, 4 * 512 + 3 * D_MODEL + 2 * D_MODEL), F32)
              + 4 * _nbytes(mk.shape[1:], F32) + _nbytes((3 * 512 + D_MODEL, D_MODEL), BF16))
    return pl.pallas_call(
        _mix_prompt_kernel,
        grid=(m // tm,),
        in_specs=[rows(512), rows(512), rows(512), rows(3 * D_MODEL), rows(512), rows(D_MODEL),
                  mem_spec, mem_spec,
                  _layer_resident(sw, layer), _resident(sb_t.shape),
                  _layer_resident(wa, layer), _layer_resident(wb, layer),
                  _layer_resident(wc, layer), _layer_resident(wo, layer)],
        out_specs=rows(D_MODEL),
        out_shape=jax.ShapeDtypeStruct((m, D_MODEL), F32),
        compiler_params=pltpu.CompilerParams(
            dimension_semantics=("parallel",), vmem_limit_bytes=_vmem_limit(blocks)),
        name="mix_prompt",
    )(u, vn, cq, gate, yb, x, mk, mv, sw, sb_t, wa, wb, wc, wo)


def _premix_sample_kernel(u_ref, vn_ref, cq_ref, mk_ref, mv_ref, wd_ref, bl_ref, ya_ref, yc_ref):
    vn = vn_ref[...]
    y = bl_ref[...] + wd_ref[0] * vn
    for d in range(1, wd_ref.shape[0]):
        y = y + wd_ref[d] * pltpu.roll(vn, d, 0)
    ya_ref[...] = u_ref[...] * y

    q_rows = MEM_HEADS * SAMPLE_ROWS
    n_keys = mk_ref.shape[1]
    row_head = lax.broadcasted_iota(jnp.int32, (q_rows, n_keys), 0) // SAMPLE_ROWS
    key_head = lax.broadcasted_iota(jnp.int32, (q_rows, n_keys), 1) % MEM_HEADS
    same_head = row_head == key_head
    scale = MEM_HEAD_DIM ** -0.5
    for b in range(mk_ref.shape[0]):
        rs = slice(b * SAMPLE_ROWS, (b + 1) * SAMPLE_ROWS)
        q8 = cq_ref[rs, :]
        q_h = jnp.concatenate([q8[:, h * MEM_HEAD_DIM:(h + 1) * MEM_HEAD_DIM]
                               for h in range(MEM_HEADS)], axis=0)
        s = _dot_nt(q_h.astype(BF16), mk_ref[b].astype(BF16)) * scale
        p, inv_l = _softmax_rows(jnp.where(same_head, s, MASK_VALUE))
        o = _dot(p.astype(BF16), mv_ref[b].astype(BF16)) * inv_l
        yc_ref[rs, :] = jnp.concatenate([o[h * SAMPLE_ROWS:(h + 1) * SAMPLE_ROWS, :]
                                         for h in range(MEM_HEADS)], axis=1)


def _premix_sample(u, vn, cq, mk, mv, wd, bl, *, layer, n_b):
    m = u.shape[0]
    tm = n_b * SAMPLE_ROWS
    rows = pl.BlockSpec((tm, 512), lambda i: (i, 0))
    mem_block = (n_b,) + mk.shape[2:]
    mem_spec = pl.BlockSpec((None,) + mem_block, lambda i: (layer, i, 0, 0))
    blocks = (2 * 5 * _nbytes((tm, 512), F32) + 4 * _nbytes(mem_block, F32)
              + _nbytes(wd.shape, F32) + _nbytes(bl.shape, F32))
    return pl.pallas_call(
        _premix_sample_kernel,
        grid=(m // tm,),
        in_specs=[rows, rows, rows, mem_spec, mem_spec, _resident(wd.shape), _resident(bl.shape)],
        out_specs=[rows, rows],
        out_shape=[jax.ShapeDtypeStruct((m, 512), F32)] * 2,
        compiler_params=pltpu.CompilerParams(
            dimension_semantics=("parallel",), vmem_limit_bytes=_vmem_limit(blocks)),
        name="premix_sample",
    )(u, vn, cq, mk, mv, wd, bl)


def _merge_kernel(x_ref, ya_ref, yb_ref, yc_ref, gate_ref, wa_ref, wb_ref, wc_ref, wo_ref, o_ref):
    o_ref[...] = _merge(x_ref[...], ya_ref[...], yb_ref[...], yc_ref[...], gate_ref,
                        wa_ref, wb_ref, wc_ref, wo_ref)


def _merge_sample(x, ya, yb, yc, gate, wa, wb, wc, wo, *, layer):
    m = x.shape[0]
    acts, weights = (x, ya, yb, yc, gate), (wa, wb, wc, wo)
    args = acts + weights
    blocks = (sum(_nbytes(a.shape, a.dtype) for a in acts) + _nbytes(x.shape, F32)
              + sum(_nbytes(w.shape[1:], w.dtype) for w in weights))
    return pl.pallas_call(
        _merge_kernel,
        grid=(1,),
        in_specs=[_resident(a.shape) for a in acts] + [_layer_resident(w, layer) for w in weights],
        out_specs=pl.BlockSpec((m, D_MODEL), lambda i: (0, 0)),
        out_shape=jax.ShapeDtypeStruct((m, D_MODEL), F32),
        compiler_params=pltpu.CompilerParams(
            dimension_semantics=("arbitrary",), vmem_limit_bytes=_vmem_limit(blocks)),
        name="merge_sample",
    )(*args)


_PAGES_PER_CHUNK = 8
_RING_SLOTS = 4
_RING_AHEAD = _RING_SLOTS - 1


def _moba_sample_kernel(pt_ref, q_ref, kn_ref, vnew_ref, ck_hbm, cv_hbm, o_ref,
                        buf, sem, s_ref, *, page_base, n_pages, page_size, n_q):
    b = pl.program_id(0)
    n_req = pl.num_programs(0)
    chunk_keys = _PAGES_PER_CHUNK * page_size
    n_chunks = n_pages // _PAGES_PER_CHUNK
    blocks_per_chunk = chunk_keys // MOBA_BLOCK
    n_past_blk = n_chunks * blocks_per_chunk
    past_len = n_pages * page_size
    q_rows = n_q * MOBA_HEADS

    def copies(req, c, slot):
        src = ck_hbm if c < n_chunks else cv_hbm
        first = req * n_pages + (c % n_chunks) * _PAGES_PER_CHUNK
        return [pltpu.make_async_copy(src.at[pt_ref[first + p] + page_base],
                                      buf.at[slot, :, pl.ds(p * page_size, page_size)],
                                      sem.at[slot])
                for p in range(_PAGES_PER_CHUNK)]

    def start(req, c, slot):
        for cp in copies(req, c, slot):
            cp.start()

    def wait(req, c, slot):
        for cp in copies(req, c, slot):
            cp.wait()

    n_total = 2 * n_chunks

    @pl.when(b == 0)
    def _():
        for c0 in range(_RING_AHEAD):
            start(b, c0, c0 % _RING_SLOTS)

    sub_i = lax.broadcasted_iota(jnp.int32, (SAMPLE_ROWS, MOBA_DIM), 0)
    own_head = sub_i == lax.broadcasted_iota(jnp.int32, (SAMPLE_ROWS, MOBA_DIM), 1) // MOBA_HEAD_DIM
    q8 = q_ref[...] * (MOBA_HEAD_DIM ** -0.5)
    q_exp = jnp.concatenate(
        [jnp.where(own_head, jnp.broadcast_to(q8[t:t + 1, :], (MOBA_HEADS, MOBA_DIM)), 0.0)
         for t in range(n_q)], axis=0)
    q_exp_b = q_exp.astype(BF16)

    def pad_rows(x8):
        return jnp.concatenate([x8, jnp.zeros((LANES - SAMPLE_ROWS, x8.shape[1]), F32)], axis=0)

    blk_lane = lax.broadcasted_iota(jnp.int32, (MOBA_DIM, LANES), 1)
    ksum = jnp.zeros((MOBA_DIM, LANES), F32)
    acc = jnp.zeros((q_rows, MOBA_DIM), F32)
    inv_l = None
    for c in range(n_total):
        slot = c % _RING_SLOTS
        wait(b, c, slot)
        nxt = c + _RING_AHEAD
        if nxt < n_total:
            start(b, nxt, nxt % _RING_SLOTS)
        else:
            @pl.when(b + 1 < n_req)
            def _(nxt=nxt):
                start(b + 1, nxt - n_total, nxt % _RING_SLOTS)

        if c < n_chunks:
            kc = buf[slot]
            for j in range(blocks_per_chunk):
                col = jnp.sum(kc[:, j * MOBA_BLOCK:(j + 1) * MOBA_BLOCK], axis=1, keepdims=True)
                ksum = jnp.where(blk_lane == c * blocks_per_chunk + j, col, ksum)
            s_ref[:, c * chunk_keys:(c + 1) * chunk_keys] = _dot(q_exp_b, kc.astype(BF16))
            if c == n_chunks - 1:
                gate = jnp.dot(q_exp, ksum * (1.0 / MOBA_BLOCK), precision=lax.Precision.HIGHEST,
                               preferred_element_type=F32)
                lane = lax.broadcasted_iota(jnp.int32, gate.shape, 1)
                rank = jnp.zeros(gate.shape, F32)
                for m in range(n_past_blk):
                    gm = gate[:, m:m + 1]
                    beats = (gm > gate) | ((gm == gate) & (lane > m))
                    rank = rank + beats.astype(F32)
                sel = ((lane < n_past_blk) & (rank < MOBA_TOPK)).astype(F32)
                s_own = _dot_nt(q_exp_b, pad_rows(kn_ref[...]).astype(BF16))
                t_i = lax.broadcasted_iota(jnp.int32, s_own.shape, 0) // MOBA_HEADS
                s_own = jnp.where(lane <= t_i, s_own, MASK_VALUE)
                mx_acc = jnp.full((q_rows, MOBA_BLOCK), MASK_VALUE, F32)
                for n in range(n_past_blk):
                    cs = slice(n * MOBA_BLOCK, (n + 1) * MOBA_BLOCK)
                    blk = jnp.where(sel[:, n:n + 1] > 0.5, s_ref[:, cs], MASK_VALUE)
                    s_ref[:, cs] = blk
                    mx_acc = jnp.maximum(mx_acc, blk)
                mx = jnp.maximum(jnp.max(mx_acc, axis=1, keepdims=True),
                                 jnp.max(s_own, axis=1, keepdims=True))
                p_own = jnp.exp(s_own - mx)
                l_acc = jnp.zeros((q_rows, MOBA_BLOCK), F32)
                for n in range(n_past_blk):
                    cs = slice(n * MOBA_BLOCK, (n + 1) * MOBA_BLOCK)
                    blk = jnp.exp(s_ref[:, cs] - mx)
                    s_ref[:, cs] = blk
                    l_acc = l_acc + blk
                s_ref[:, past_len:past_len + LANES] = p_own
                inv_l = 1.0 / (jnp.sum(l_acc, axis=1, keepdims=True)
                               + jnp.sum(p_own, axis=1, keepdims=True))
        else:
            cc = c - n_chunks
            p_c = s_ref[:, cc * chunk_keys:(cc + 1) * chunk_keys].astype(BF16)
            acc = acc + _dot_nt(p_c, buf[slot].astype(BF16))

    p_own = s_ref[:, past_len:past_len + LANES].astype(BF16)
    acc = (acc + _dot(p_own, pad_rows(vnew_ref[...]).astype(BF16))) * inv_l
    y = jnp.zeros((SAMPLE_ROWS, MOBA_DIM), F32)
    for t in range(n_q):
        grp = jnp.where(own_head, acc[t * MOBA_HEADS:(t + 1) * MOBA_HEADS, :], 0.0)
        y = jnp.where(sub_i == t, jnp.sum(grp, axis=0, keepdims=True), y)
    o_ref[...] = y


def _moba_sample(page_table, q, k_new, v_new, cache_k, cache_v, *, layer, n_q):
    n_req, n_pages = page_table.shape
    depth, n_pool, page_size = cache_k.shape[:3]
    ck = jnp.transpose(cache_k, (0, 1, 3, 4, 2)).reshape(depth * n_pool, MOBA_DIM, page_size)
    cv = jnp.transpose(cache_v, (0, 1, 3, 4, 2)).reshape(depth * n_pool, MOBA_DIM, page_size)
    chunk_keys = _PAGES_PER_CHUNK * page_size
    past_len = n_pages * page_size
    q_rows = n_q * MOBA_HEADS
    rows = pl.BlockSpec((SAMPLE_ROWS, MOBA_DIM), lambda b, pt: (b, 0))
    hbm = pl.BlockSpec(memory_space=pl.ANY)
    assert (2 * n_pages // _PAGES_PER_CHUNK) % _RING_SLOTS == 0
    scratch = [pltpu.VMEM((_RING_SLOTS, MOBA_DIM, chunk_keys), F32),
               pltpu.SemaphoreType.DMA((_RING_SLOTS,)),
               pltpu.VMEM((q_rows, past_len + LANES), F32)]
    blocks = (_nbytes((_RING_SLOTS, MOBA_DIM, chunk_keys), F32) + _nbytes((q_rows, past_len + LANES), F32)
              + 8 * _nbytes((SAMPLE_ROWS, MOBA_DIM), F32))
    return pl.pallas_call(
        functools.partial(_moba_sample_kernel, page_base=layer * n_pool, n_pages=n_pages,
                          page_size=page_size, n_q=n_q),
        grid_spec=pltpu.PrefetchScalarGridSpec(
            num_scalar_prefetch=1,
            grid=(n_req,),
            in_specs=[rows, rows, rows, hbm, hbm],
            out_specs=rows,
            scratch_shapes=scratch),
        out_shape=jax.ShapeDtypeStruct((n_req * SAMPLE_ROWS, MOBA_DIM), F32),
        compiler_params=pltpu.CompilerParams(
            dimension_semantics=("arbitrary",), vmem_limit_bytes=_vmem_limit(blocks)),
        name="moba_sample",
    )(page_table.reshape(-1), q, k_new, v_new, ck, cv)


def _sample_sgu_tables(sgu_w, sgu_b, n_b, t_new):
    t = jnp.arange(SAMPLE_ROWS)
    d = jnp.arange(t_new)
    src = t[None, :] - d[:, None]
    ok = (src >= 0) & (t[None, :] < t_new)
    w_td = sgu_w[:, t[None, :].clip(0, t_new - 1), src.clip(0, t_new - 1)]
    w_td = jnp.where(ok[None], w_td, 0.0)
    wd = jnp.repeat(jnp.transpose(w_td, (1, 2, 0)), SGU_GROUP_DIM, axis=2)
    bl = jnp.where((t < t_new)[:, None],
                   jnp.repeat(sgu_b[:, t.clip(0, t_new - 1)].T, SGU_GROUP_DIM, axis=1), 0.0)
    return jnp.tile(wd, (1, n_b, 1)), jnp.tile(bl, (n_b, 1))


def kernel(x_prompt, x_sample, mem_prompt, cache_k, cache_v, cache_mem_k, cache_mem_v, page_table,
           ffn1_norm, ffn1_w1, ffn1_w3, ffn1_w2, mix_norm, w_in, sgu_ln_g, sgu_ln_b, sgu_w, sgu_b,
           mem_norm, w_mem_kv, w_a_out, w_b_out, w_c_out, w_o, ffn2_norm, ffn2_w1, ffn2_w3, ffn2_w2,
           final_norm):
    depth = w_in.shape[0]
    bsz, t_len, _ = x_prompt.shape
    dec_b, t_new, _ = x_sample.shape
    n_pages = page_table.shape[1]
    page_size = cache_k.shape[2]
    past_len = n_pages * page_size
    mem_len = mem_prompt.shape[1]
    assert t_len % 512 == 0 and t_new <= SAMPLE_ROWS and past_len % MOBA_BLOCK == 0
    assert n_pages % _PAGES_PER_CHUNK == 0 and (_PAGES_PER_CHUNK * page_size) % MOBA_BLOCK == 0

    row = lambda a: a.reshape(1, -1)
    fin = row(final_norm)
    f1 = tuple(w.astype(BF16) for w in (ffn1_w1, ffn1_w3, ffn1_w2))
    f2 = tuple(w.astype(BF16) for w in (ffn2_w1, ffn2_w3, ffn2_w2))
    win, wkv = w_in.astype(BF16), w_mem_kv.astype(BF16)
    outs_w = tuple(w.astype(BF16) for w in (w_a_out, w_b_out, w_c_out, w_o))

    tm_p = 512
    cos_p, slo_p, shi_p = _rope_tables(jnp.arange(t_len, dtype=jnp.int32))
    h = x_prompt.reshape(bsz * t_len, D_MODEL)
    mem = mem_prompt.reshape(bsz * mem_len, D_MODEL)
    kp, vp, mkp, mvp = [], [], [], []
    for l in range(depth):
        mk, mv = _memkv(mem, row(mem_norm[l]), wkv, layer=l, tm=512)
        h = _ffn(h, row(ffn1_norm[l]), *f1, fin, layer=l, tm=tm_p, final_norm=False)
        u, vn, q, k, v, cq, gate, k_t, v_t = _inproj(
            h, row(mix_norm[l]), win, row(sgu_ln_g[l]), row(sgu_ln_b[l]), cos_p, slo_p, shi_p,
            layer=l, tm=tm_p, t_major_kv=True)
        yb = _moba_prompt(q.reshape(bsz, t_len, MOBA_DIM), k.reshape(bsz, t_len, MOBA_DIM),
                          v.reshape(bsz, t_len, MOBA_DIM)).reshape(bsz * t_len, MOBA_DIM)
        h = _mix_prompt(u, vn, cq, gate, yb, h, mk.reshape(bsz, mem_len, MEM_DIM),
                        mv.reshape(bsz, mem_len, MEM_DIM), sgu_w, sgu_b[l].T, *outs_w,
                        layer=l, tm=tm_p, t_len=t_len)
        h = _ffn(h, row(ffn2_norm[l]), *f2, fin, layer=l, tm=tm_p, final_norm=(l == depth - 1))
        kp.append(k_t)
        vp.append(v_t)
        mkp.append(mk.reshape(bsz, mem_len, MEM_HEADS, MEM_HEAD_DIM))
        mvp.append(mv.reshape(bsz, mem_len, MEM_HEADS, MEM_HEAD_DIM))
    y_prompt = h.reshape(bsz, t_len, D_MODEL)

    def token_minor_to_rows(ts):
        stacked = jnp.stack(ts).reshape(depth, bsz, MOBA_HEADS, MOBA_HEAD_DIM, t_len)
        return jnp.transpose(stacked, (0, 1, 4, 2, 3))

    m_s = dec_b * SAMPLE_ROWS
    pos_s = past_len + jnp.arange(SAMPLE_ROWS, dtype=jnp.int32)
    cos_s, slo_s, shi_s = (jnp.tile(t, (dec_b, 1)) for t in _rope_tables(pos_s))
    h = jnp.pad(x_sample, ((0, 0), (0, SAMPLE_ROWS - t_new), (0, 0))).reshape(m_s, D_MODEL)
    n_b = 8
    mem_k = cache_mem_k.reshape(depth, dec_b, mem_len * MEM_HEADS, MEM_HEAD_DIM)
    mem_v = cache_mem_v.reshape(depth, dec_b, mem_len * MEM_HEADS, MEM_HEAD_DIM)
    ks_, vs_, sv_ = [], [], []
    unpad = lambda a: a.reshape(dec_b, SAMPLE_ROWS, -1)[:, :t_new]
    for l in range(depth):
        wd, bl = _sample_sgu_tables(sgu_w[l], sgu_b[l], n_b, t_new)
        h = _ffn(h, row(ffn1_norm[l]), *f1, fin, layer=l, tm=m_s, final_norm=False)
        u, vn, q, k, v, cq, gate = _inproj(
            h, row(mix_norm[l]), win, row(sgu_ln_g[l]), row(sgu_ln_b[l]), cos_s, slo_s, shi_s,
            layer=l, tm=m_s, t_major_kv=False)
        ya, yc = _premix_sample(u, vn, cq, mem_k, mem_v, wd, bl, layer=l, n_b=n_b)
        yb = _moba_sample(page_table, q, k, v, cache_k, cache_v, layer=l, n_q=t_new)
        h = _merge_sample(h, ya, yb, yc, gate, *outs_w, layer=l)
        h = _ffn(h, row(ffn2_norm[l]), *f2, fin, layer=l, tm=m_s, final_norm=(l == depth - 1))
        ks_.append(unpad(k).reshape(dec_b, t_new, MOBA_HEADS, MOBA_HEAD_DIM))
        vs_.append(unpad(v).reshape(dec_b, t_new, MOBA_HEADS, MOBA_HEAD_DIM))
        sv_.append(unpad(vn))
    y_sample = unpad(h)

    return (y_prompt, y_sample, token_minor_to_rows(kp), token_minor_to_rows(vp),
            jnp.stack(mkp), jnp.stack(mvp), jnp.stack(ks_), jnp.stack(vs_), jnp.stack(sv_))
```

```python
import functools

import jax
import jax.numpy as jnp
import numpy as np
from jax import lax
from jax.experimental import pallas as pl
from jax.experimental.pallas import tpu as pltpu

F32 = jnp.float32
BF16 = jnp.bfloat16

D_MODEL = 1024
SGU_GROUPS = 4
SGU_GROUP_DIM = 128
SGU_DIM = 512
CHUNK = 128
MOBA_HEADS = 8
MOBA_HEAD_DIM = 64
MOBA_DIM = 512
MOBA_BLOCK = 256
MOBA_TOPK = 3
MEM_HEADS = 4
MEM_HEAD_DIM = 128
MEM_DIM = 512
D_FF = 2816
ROPE_THETA = 10000.0
NORM_EPS = 1e-6
MASK_VALUE = -1e30
LOG2_E = float(np.log2(np.e))
IN_COLS = 6144
GATE_COL0 = 3072

V7X_VMEM_BYTES = 64 * 1024 * 1024
LANES = 128
SUBLANES = 8
SAMPLE_ROWS = 8
TEMP_ALLOWANCE = 12 * 1024 * 1024

NT_DIMS = (((1,), (1,)), ((), ()))


def _vmem_limit(block_bytes):
    return int(min(block_bytes + TEMP_ALLOWANCE, V7X_VMEM_BYTES - 4 * 1024 * 1024))


def _nbytes(shape, dtype):
    return int(np.prod(shape)) * jnp.dtype(dtype).itemsize


def _resident(shape):
    nd = len(shape)
    return pl.BlockSpec(shape, lambda *_: (0,) * nd, pipeline_mode=pl.Buffered(1))


def _layer_resident(arr, layer):
    nd = arr.ndim
    return pl.BlockSpec((None,) + arr.shape[1:], lambda *_: (layer,) + (0,) * (nd - 1),
                        pipeline_mode=pl.Buffered(1))


def _dot(a, b):
    return jnp.dot(a, b, preferred_element_type=F32)


def _dot_nt(a, b, precision=None):
    return lax.dot_general(a, b, NT_DIMS, precision=precision, preferred_element_type=F32)


def _rms(x, g):
    return x * lax.rsqrt(jnp.mean(x * x, axis=-1, keepdims=True) + NORM_EPS) * g


def _sigmoid(x):
    return 1.0 / (1.0 + jnp.exp(-x))


def _gelu(x):
    return 0.5 * x * (1.0 + lax.erf(x * np.float32(np.sqrt(0.5))))


def _softmax_rows(s):
    p = jnp.exp(s - jnp.max(s, axis=-1, keepdims=True))
    return p, 1.0 / jnp.sum(p, axis=-1, keepdims=True)


_FF_CHUNK = 512
_FF_CHUNKS = tuple((s, min(_FF_CHUNK, D_FF - s)) for s in range(0, D_FF, _FF_CHUNK))


def _ffn_kernel(x_ref, g_ref, w1_ref, w3_ref, w2_ref, fg_ref, o_ref, a_ref, *, final_norm):
    x = x_ref[...]
    xb = _rms(x, g_ref[...]).astype(BF16)
    for s, w in _FF_CHUNKS:
        h1 = _dot(xb, w1_ref[:, s:s + w])
        h3 = _dot(xb, w3_ref[:, s:s + w])
        a_ref[:, s:s + w] = (h1 * _sigmoid(h1) * h3).astype(BF16)
    y = x + 0.5 * _dot(a_ref[...], w2_ref[...])
    if final_norm:
        y = _rms(y, fg_ref[...])
    o_ref[...] = y


def _ffn(x, g, w1, w3, w2, fg, *, layer, tm, final_norm):
    m = x.shape[0]
    blocks = (2 * 2 * _nbytes((tm, D_MODEL), F32) + 3 * _nbytes((D_MODEL, D_FF), BF16)
              + _nbytes((tm, D_FF), BF16))
    return pl.pallas_call(
        functools.partial(_ffn_kernel, final_norm=final_norm),
        grid=(m // tm,),
        in_specs=[
            pl.BlockSpec((tm, D_MODEL), lambda i: (i, 0)),
            _resident((1, D_MODEL)),
            _layer_resident(w1, layer),
            _layer_resident(w3, layer),
            _layer_resident(w2, layer),
            _resident((1, D_MODEL)),
        ],
        out_specs=pl.BlockSpec((tm, D_MODEL), lambda i: (i, 0)),
        out_shape=jax.ShapeDtypeStruct((m, D_MODEL), F32),
        scratch_shapes=[pltpu.VMEM((tm, D_FF), BF16)],
        compiler_params=pltpu.CompilerParams(
            dimension_semantics=("parallel",), vmem_limit_bytes=_vmem_limit(blocks)),
        name="ffn",
    )(x, g, w1, w3, w2, fg)


def _rope(h, cos, sin_lo, sin_hi):
    outs = []
    for c in range(h.shape[1] // LANES):
        s = h[:, c * LANES:(c + 1) * LANES]
        outs.append(s * cos + pltpu.roll(s, LANES - 32, 1) * sin_lo + pltpu.roll(s, 32, 1) * sin_hi)
    return jnp.concatenate(outs, axis=1)


def _inproj_kernel(x_ref, g_ref, w_ref, lng_ref, lnb_ref, cos_ref, slo_ref, shi_ref,
                   u_ref, vn_ref, q_ref, k_ref, v_ref, cq_ref, gate_ref, *t_refs):
    xb = _rms(x_ref[...], g_ref[...]).astype(BF16)

    def proj(c0, w=512):
        return _dot(xb, w_ref[:, c0:c0 + w])

    u_ref[...] = _gelu(proj(0)).astype(u_ref.dtype)
    a = _gelu(proj(512))
    ac = a - jnp.mean(a, axis=-1, keepdims=True)
    vn_ref[...] = (ac * lax.rsqrt(jnp.mean(ac * ac, axis=-1, keepdims=True) + NORM_EPS)
                   * lng_ref[...] + lnb_ref[...]).astype(vn_ref.dtype)
    cos, slo, shi = cos_ref[...], slo_ref[...], shi_ref[...]
    q_ref[...] = _rope(proj(1024), cos, slo, shi)
    k = _rope(proj(1536), cos, slo, shi)
    v = proj(2048)
    k_ref[...] = k
    v_ref[...] = v
    if t_refs:
        t_refs[0][...] = k.T
        t_refs[1][...] = v.T
    cq_ref[...] = proj(2560).astype(cq_ref.dtype)
    for c in range(6):
        gate_ref[:, c * 512:(c + 1) * 512] = _sigmoid(proj(GATE_COL0 + c * 512)).astype(gate_ref.dtype)


def _inproj(x, g, w, lng, lnb, cos, slo, shi, *, layer, tm, t_major_kv, act_dtype):
    m = x.shape[0]
    t_len = cos.shape[0]
    n_tab = t_len // tm
    out_cols = (512,) * 6 + (3 * D_MODEL,)
    blocks = (2 * _nbytes((tm, D_MODEL), F32) + _nbytes((D_MODEL, IN_COLS), BF16)
              + 2 * _nbytes((tm, IN_COLS), F32) + 6 * _nbytes((tm, LANES), F32))
    tab = pl.BlockSpec((tm, LANES), lambda i: (i % n_tab, 0))
    out_specs = [pl.BlockSpec((tm, c), lambda i: (i, 0)) for c in out_cols]
    out_dtypes = (act_dtype, act_dtype, F32, F32, F32, act_dtype, act_dtype)
    out_shape = [jax.ShapeDtypeStruct((m, c), dt) for c, dt in zip(out_cols, out_dtypes)]
    if t_major_kv:
        blocks += 4 * _nbytes((MOBA_DIM, tm), F32)
        out_specs += [pl.BlockSpec((None, MOBA_DIM, tm), lambda i: (i // n_tab, 0, i % n_tab))] * 2
        out_shape += [jax.ShapeDtypeStruct((m // t_len, MOBA_DIM, t_len), F32)] * 2
    return pl.pallas_call(
        _inproj_kernel,
        grid=(m // tm,),
        in_specs=[
            pl.BlockSpec((tm, D_MODEL), lambda i: (i, 0)),
            _resident((1, D_MODEL)),
            _layer_resident(w, layer),
            _resident((1, SGU_DIM)),
            _resident((1, SGU_DIM)),
            tab, tab, tab,
        ],
        out_specs=out_specs,
        out_shape=out_shape,
        compiler_params=pltpu.CompilerParams(
            dimension_semantics=("parallel",), vmem_limit_bytes=_vmem_limit(blocks)),
        name="inproj",
    )(x, g, w, lng, lnb, cos, slo, shi)


def _rope_tables(pos):
    half = MOBA_HEAD_DIM // 2
    inv = ROPE_THETA ** (-jnp.arange(half, dtype=F32) * 2.0 / MOBA_HEAD_DIM)
    ang = pos.astype(F32)[:, None] * inv[None, :]
    cos, sin = jnp.tile(jnp.cos(ang), (1, 4)), jnp.tile(jnp.sin(ang), (1, 4))
    first_half = (jnp.arange(LANES) % MOBA_HEAD_DIM) < half
    return cos, jnp.where(first_half, -sin, 0.0), jnp.where(first_half, 0.0, sin)


def _memkv_kernel(m_ref, g_ref, w_ref, k_ref, v_ref):
    xb = _rms(m_ref[...], g_ref[...]).astype(BF16)
    k_ref[...] = _dot(xb, w_ref[:, :MEM_DIM])
    v_ref[...] = _dot(xb, w_ref[:, MEM_DIM:])


def _memkv(mem, g, w, *, layer, tm):
    m = mem.shape[0]
    blocks = (2 * _nbytes((tm, D_MODEL), F32) + _nbytes((D_MODEL, 2 * MEM_DIM), BF16)
              + 4 * _nbytes((tm, MEM_DIM), F32))
    return pl.pallas_call(
        _memkv_kernel,
        grid=(m // tm,),
        in_specs=[pl.BlockSpec((tm, D_MODEL), lambda i: (i, 0)), _resident((1, D_MODEL)),
                  _layer_resident(w, layer)],
        out_specs=[pl.BlockSpec((tm, MEM_DIM), lambda i: (i, 0))] * 2,
        out_shape=[jax.ShapeDtypeStruct((m, MEM_DIM), F32)] * 2,
        compiler_params=pltpu.CompilerParams(
            dimension_semantics=("parallel",), vmem_limit_bytes=_vmem_limit(blocks)),
        name="memkv",
    )(mem, g, w)


def _topk_select(g, n_past, n_rows):
    row = lax.broadcasted_iota(jnp.int32, g.shape, 0)
    rank = jnp.zeros(g.shape, F32)
    for m in range(n_past):
        gm = g[m:m + 1, :]
        beats = (gm > g) | ((gm == g) & (row > m))
        rank = rank + beats.astype(F32)
    return (row < n_past) & (rank < MOBA_TOPK)


_SCORES_AHEAD = 1


def _moba_prompt_kernel(q_ref, k_ref, v_ref, o_ref):
    t_len = q_ref.shape[1]
    n_blk = t_len // MOBA_BLOCK
    k = k_ref[0]
    kb = k.astype(BF16)
    v_aug = jnp.concatenate([v_ref[0].astype(BF16), jnp.ones((t_len, LANES), BF16)], axis=1)
    k_mean = jnp.sum(k.reshape(n_blk, MOBA_BLOCK, LANES), axis=1) * (1.0 / MOBA_BLOCK)
    lane = lax.broadcasted_iota(jnp.int32, (1, LANES), 1)
    head_masks = (lane < MOBA_HEAD_DIM, lane >= MOBA_HEAD_DIM)
    r_i = lax.broadcasted_iota(jnp.int32, (2 * MOBA_BLOCK, MOBA_BLOCK), 0) & (MOBA_BLOCK - 1)
    c_i = lax.broadcasted_iota(jnp.int32, (2 * MOBA_BLOCK, MOBA_BLOCK), 1)
    causal2 = c_i <= r_i
    scale = MOBA_HEAD_DIM ** -0.5 * LOG2_E

    def scores(qi):
        q_blk = q_ref[0, qi * MOBA_BLOCK:(qi + 1) * MOBA_BLOCK, :]
        q2 = jnp.concatenate([jnp.where(hm, q_blk, 0.0) for hm in head_masks], axis=0)
        return q2, _dot_nt((q2 * scale).astype(BF16), kb[:(qi + 1) * MOBA_BLOCK])

    def weighted_values(qi, p):
        o2 = _dot(p, v_aug[:(qi + 1) * MOBA_BLOCK])
        o2 = o2[:, :LANES] * (1.0 / o2[:, LANES:])
        o_ref[0, qi * MOBA_BLOCK:(qi + 1) * MOBA_BLOCK, :] = jnp.where(
            head_masks[0], o2[:MOBA_BLOCK], o2[MOBA_BLOCK:]).astype(o_ref.dtype)

    ahead = [scores(qi) for qi in range(min(_SCORES_AHEAD, n_blk))]
    pending = None
    for qi in range(n_blk):
        t_k = (qi + 1) * MOBA_BLOCK
        q2, s = ahead.pop(0)
        if qi + _SCORES_AHEAD < n_blk:
            ahead.append(scores(qi + _SCORES_AHEAD))
        parts = []
        if qi > MOBA_TOPK:
            gate = _dot_nt(k_mean, q2, precision=lax.Precision.HIGHEST)
            sel = _topk_select(gate, qi, n_blk).astype(F32)
            sel_t = jnp.concatenate(
                [sel, jnp.zeros((LANES - n_blk, 2 * MOBA_BLOCK), F32)], axis=0).T
            for n in range(qi):
                keep = sel_t[:, n:n + 1] > 0.5
                parts.append(jnp.where(keep, s[:, n * MOBA_BLOCK:(n + 1) * MOBA_BLOCK], MASK_VALUE))
        else:
            for n in range(qi):
                parts.append(s[:, n * MOBA_BLOCK:(n + 1) * MOBA_BLOCK])
        parts.append(jnp.where(causal2, s[:, qi * MOBA_BLOCK:t_k], MASK_VALUE))
        sm = jnp.concatenate(parts, axis=1) if len(parts) > 1 else parts[0]
        p = jnp.exp2(sm - jnp.max(sm, axis=-1, keepdims=True)).astype(BF16)
        if pending is not None:
            weighted_values(*pending)
        pending = (qi, p)
    weighted_values(*pending)


def _moba_prompt(q, k, v, *, out_dtype):
    bsz, t_len, _ = q.shape
    spec = pl.BlockSpec((1, t_len, LANES), lambda b, h: (b, 0, h))
    blocks = (8 * _nbytes((t_len, LANES), F32) + 2 * _nbytes((t_len, LANES), BF16)
              + 8 * _nbytes((MOBA_BLOCK, t_len), F32))
    return pl.pallas_call(
        _moba_prompt_kernel,
        grid=(bsz, MOBA_DIM // LANES),
        in_specs=[spec, spec, spec],
        out_specs=spec,
        out_shape=jax.ShapeDtypeStruct((bsz, t_len, MOBA_DIM), out_dtype),
        compiler_params=pltpu.CompilerParams(
            dimension_semantics=("parallel", "parallel"), vmem_limit_bytes=_vmem_limit(blocks)),
        name="moba_prompt",
    )(q, k, v)


def _merge(x, ya, yb, yc, gate_ref, wa_ref, wb_ref, wc_ref, wo_ref):
    t = (gate_ref[:, 0:D_MODEL] * _dot(ya.astype(BF16), wa_ref[...])
         + gate_ref[:, D_MODEL:2 * D_MODEL] * _dot(yb.astype(BF16), wb_ref[...])
         + gate_ref[:, 2 * D_MODEL:3 * D_MODEL] * _dot(yc.astype(BF16), wc_ref[...]))
    return x + _dot(t.astype(BF16), wo_ref[...])


def _mem_attention(cq, mkb, mvb):
    scale = MEM_HEAD_DIM ** -0.5
    outs = []
    for h in range(MEM_HEADS):
        sl = slice(h * MEM_HEAD_DIM, (h + 1) * MEM_HEAD_DIM)
        s = _dot_nt(cq[:, sl].astype(BF16), mkb[:, sl]) * scale
        p, inv_l = _softmax_rows(s)
        outs.append(_dot(p.astype(BF16), mvb[:, sl]) * inv_l)
    return jnp.concatenate(outs, axis=1)


def _mix_prompt_kernel(u_ref, vn_ref, cq_ref, gate_ref, yb_ref, x_ref, mk_ref, mv_ref,
                       sw_ref, sb_ref, wa_ref, wb_ref, wc_ref, wo_ref, o_ref):
    tm = u_ref.shape[0]
    r_i = lax.broadcasted_iota(jnp.int32, (CHUNK, CHUNK), 0)
    c_i = lax.broadcasted_iota(jnp.int32, (CHUNK, CHUNK), 1)
    wm = [jnp.where(c_i <= r_i, sw_ref[g], 0.0).astype(BF16) for g in range(SGU_GROUPS)]
    rows = []
    for ch in range(tm // CHUNK):
        rs = slice(ch * CHUNK, (ch + 1) * CHUNK)
        cols = []
        for g in range(SGU_GROUPS):
            cs = slice(g * SGU_GROUP_DIM, (g + 1) * SGU_GROUP_DIM)
            y = _dot(wm[g], vn_ref[rs, cs].astype(BF16)) + sb_ref[:, g:g + 1]
            cols.append(u_ref[rs, cs].astype(F32) * y)
        rows.append(jnp.concatenate(cols, axis=1))
    ya = jnp.concatenate(rows, axis=0)
    yc = _mem_attention(cq_ref[...], mk_ref[0].astype(BF16), mv_ref[0].astype(BF16))
    o_ref[...] = _merge(x_ref[...], ya, yb_ref[...], yc, gate_ref, wa_ref, wb_ref, wc_ref, wo_ref)


def _mix_prompt(u, vn, cq, gate, yb, x, mk, mv, sw, sb_t, wa, wb, wc, wo, *, layer, tm, t_len):
    m = x.shape[0]
    per_b = t_len // tm

    def rows(c):
        return pl.BlockSpec((tm, c), lambda i: (i, 0))

    mem_spec = pl.BlockSpec((1,) + mk.shape[1:], lambda i: (i // per_b, 0, 0))
    blocks = (2 * _nbytes((tm, 4 * 512 + 3 * D_MODEL + 2 * D_MODEL), F32)
              + 4 * _nbytes(mk.shape[1:], F32) + _nbytes((3 * 512 + D_MODEL, D_MODEL), BF16))
    return pl.pallas_call(
        _mix_prompt_kernel,
        grid=(m // tm,),
        in_specs=[rows(512), rows(512), rows(512), rows(3 * D_MODEL), rows(512), rows(D_MODEL),
                  mem_spec, mem_spec,
                  _layer_resident(sw, layer), _resident(sb_t.shape),
                  _layer_resident(wa, layer), _layer_resident(wb, layer),
                  _layer_resident(wc, layer), _layer_resident(wo, layer)],
        out_specs=rows(D_MODEL),
        out_shape=jax.ShapeDtypeStruct((m, D_MODEL), F32),
        compiler_params=pltpu.CompilerParams(
            dimension_semantics=("parallel",), vmem_limit_bytes=_vmem_limit(blocks)),
        name="mix_prompt",
    )(u, vn, cq, gate, yb, x, mk, mv, sw, sb_t, wa, wb, wc, wo)


def _premix_sample_kernel(u_ref, vn_ref, cq_ref, mk_ref, mv_ref, wd_ref, bl_ref, ya_ref, yc_ref):
    vn = vn_ref[...]
    y = bl_ref[...] + wd_ref[0] * vn
    for d in range(1, wd_ref.shape[0]):
        y = y + wd_ref[d] * pltpu.roll(vn, d, 0)
    ya_ref[...] = u_ref[...] * y

    q_rows = MEM_HEADS * SAMPLE_ROWS
    n_keys = mk_ref.shape[1]
    row_head = lax.broadcasted_iota(jnp.int32, (q_rows, n_keys), 0) // SAMPLE_ROWS
    key_head = lax.broadcasted_iota(jnp.int32, (q_rows, n_keys), 1) % MEM_HEADS
    same_head = row_head == key_head
    scale = MEM_HEAD_DIM ** -0.5
    for b in range(mk_ref.shape[0]):
        rs = slice(b * SAMPLE_ROWS, (b + 1) * SAMPLE_ROWS)
        q8 = cq_ref[rs, :]
        q_h = jnp.concatenate([q8[:, h * MEM_HEAD_DIM:(h + 1) * MEM_HEAD_DIM]
                               for h in range(MEM_HEADS)], axis=0)
        s = _dot_nt(q_h.astype(BF16), mk_ref[b].astype(BF16)) * scale
        p, inv_l = _softmax_rows(jnp.where(same_head, s, MASK_VALUE))
        o = _dot(p.astype(BF16), mv_ref[b].astype(BF16)) * inv_l
        yc_ref[rs, :] = jnp.concatenate([o[h * SAMPLE_ROWS:(h + 1) * SAMPLE_ROWS, :]
                                         for h in range(MEM_HEADS)], axis=1)


def _premix_sample(u, vn, cq, mk, mv, wd, bl, *, layer, n_b):
    m = u.shape[0]
    tm = n_b * SAMPLE_ROWS
    rows = pl.BlockSpec((tm, 512), lambda i: (i, 0))
    mem_block = (n_b,) + mk.shape[2:]
    mem_spec = pl.BlockSpec((None,) + mem_block, lambda i: (layer, i, 0, 0))
    blocks = (2 * 5 * _nbytes((tm, 512), F32) + 4 * _nbytes(mem_block, F32)
              + _nbytes(wd.shape, F32) + _nbytes(bl.shape, F32))
    return pl.pallas_call(
        _premix_sample_kernel,
        grid=(m // tm,),
        in_specs=[rows, rows, rows, mem_spec, mem_spec, _resident(wd.shape), _resident(bl.shape)],
        out_specs=[rows, rows],
        out_shape=[jax.ShapeDtypeStruct((m, 512), F32)] * 2,
        compiler_params=pltpu.CompilerParams(
            dimension_semantics=("parallel",), vmem_limit_bytes=_vmem_limit(blocks)),
        name="premix_sample",
    )(u, vn, cq, mk, mv, wd, bl)


def _merge_kernel(x_ref, ya_ref, yb_ref, yc_ref, gate_ref, wa_ref, wb_ref, wc_ref, wo_ref, o_ref):
    o_ref[...] = _merge(x_ref[...], ya_ref[...], yb_ref[...], yc_ref[...], gate_ref,
                        wa_ref, wb_ref, wc_ref, wo_ref)


def _merge_sample(x, ya, yb, yc, gate, wa, wb, wc, wo, *, layer):
    m = x.shape[0]
    acts, weights = (x, ya, yb, yc, gate), (wa, wb, wc, wo)
    args = acts + weights
    blocks = (sum(_nbytes(a.shape, a.dtype) for a in acts) + _nbytes(x.shape, F32)
              + sum(_nbytes(w.shape[1:], w.dtype) for w in weights))
    return pl.pallas_call(
        _merge_kernel,
        grid=(1,),
        in_specs=[_resident(a.shape) for a in acts] + [_layer_resident(w, layer) for w in weights],
        out_specs=pl.BlockSpec((m, D_MODEL), lambda i: (0, 0)),
        out_shape=jax.ShapeDtypeStruct((m, D_MODEL), F32),
        compiler_params=pltpu.CompilerParams(
            dimension_semantics=("arbitrary",), vmem_limit_bytes=_vmem_limit(blocks)),
        name="merge_sample",
    )(*args)


_PAGES_PER_CHUNK = 8
_RING_SLOTS = 4
_RING_AHEAD = _RING_SLOTS - 1


def _moba_sample_kernel(pt_ref, q_ref, kn_ref, vnew_ref, ck_hbm, cv_hbm, o_ref,
                        buf, sem, s_ref, *, page_base, n_pages, page_size, n_q):
    b = pl.program_id(0)
    n_req = pl.num_programs(0)
    chunk_keys = _PAGES_PER_CHUNK * page_size
    n_chunks = n_pages // _PAGES_PER_CHUNK
    blocks_per_chunk = chunk_keys // MOBA_BLOCK
    n_past_blk = n_chunks * blocks_per_chunk
    past_len = n_pages * page_size
    q_rows = n_q * MOBA_HEADS

    def copies(req, c, slot):
        src = ck_hbm if c < n_chunks else cv_hbm
        first = req * n_pages + (c % n_chunks) * _PAGES_PER_CHUNK
        return [pltpu.make_async_copy(src.at[pt_ref[first + p] + page_base],
                                      buf.at[slot, :, pl.ds(p * page_size, page_size)],
                                      sem.at[slot])
                for p in range(_PAGES_PER_CHUNK)]

    def start(req, c, slot):
        for cp in copies(req, c, slot):
            cp.start()

    def wait(req, c, slot):
        for cp in copies(req, c, slot):
            cp.wait()

    n_total = 2 * n_chunks

    @pl.when(b == 0)
    def _():
        for c0 in range(_RING_AHEAD):
            start(b, c0, c0 % _RING_SLOTS)

    sub_i = lax.broadcasted_iota(jnp.int32, (SAMPLE_ROWS, MOBA_DIM), 0)
    own_head = sub_i == lax.broadcasted_iota(jnp.int32, (SAMPLE_ROWS, MOBA_DIM), 1) // MOBA_HEAD_DIM
    q8 = q_ref[...] * (MOBA_HEAD_DIM ** -0.5)
    q_exp = jnp.concatenate(
        [jnp.where(own_head, jnp.broadcast_to(q8[t:t + 1, :], (MOBA_HEADS, MOBA_DIM)), 0.0)
         for t in range(n_q)], axis=0)
    q_exp_b = q_exp.astype(BF16)

    def pad_rows(x8):
        return jnp.concatenate([x8, jnp.zeros((LANES - SAMPLE_ROWS, x8.shape[1]), F32)], axis=0)

    blk_lane = lax.broadcasted_iota(jnp.int32, (MOBA_DIM, LANES), 1)
    ksum = jnp.zeros((MOBA_DIM, LANES), F32)
    acc = jnp.zeros((q_rows, MOBA_DIM), F32)
    inv_l = None
    for c in range(n_total):
        slot = c % _RING_SLOTS
        wait(b, c, slot)
        nxt = c + _RING_AHEAD
        if nxt < n_total:
            start(b, nxt, nxt % _RING_SLOTS)
        else:
            @pl.when(b + 1 < n_req)
            def _(nxt=nxt):
                start(b + 1, nxt - n_total, nxt % _RING_SLOTS)

        if c < n_chunks:
            kc = buf[slot]
            for j in range(blocks_per_chunk):
                col = jnp.sum(kc[:, j * MOBA_BLOCK:(j + 1) * MOBA_BLOCK], axis=1, keepdims=True)
                ksum = jnp.where(blk_lane == c * blocks_per_chunk + j, col, ksum)
            s_ref[:, c * chunk_keys:(c + 1) * chunk_keys] = _dot(q_exp_b, kc.astype(BF16))
            if c == n_chunks - 1:
                gate = jnp.dot(q_exp, ksum * (1.0 / MOBA_BLOCK), precision=lax.Precision.HIGHEST,
                               preferred_element_type=F32)
                lane = lax.broadcasted_iota(jnp.int32, gate.shape, 1)
                rank = jnp.zeros(gate.shape, F32)
                for m in range(n_past_blk):
                    gm = gate[:, m:m + 1]
                    beats = (gm > gate) | ((gm == gate) & (lane > m))
                    rank = rank + beats.astype(F32)
                sel = ((lane < n_past_blk) & (rank < MOBA_TOPK)).astype(F32)
                s_own = _dot_nt(q_exp_b, pad_rows(kn_ref[...]).astype(BF16))
                t_i = lax.broadcasted_iota(jnp.int32, s_own.shape, 0) // MOBA_HEADS
                s_own = jnp.where(lane <= t_i, s_own, MASK_VALUE)
                mx_acc = jnp.full((q_rows, MOBA_BLOCK), MASK_VALUE, F32)
                for n in range(n_past_blk):
                    cs = slice(n * MOBA_BLOCK, (n + 1) * MOBA_BLOCK)
                    blk = jnp.where(sel[:, n:n + 1] > 0.5, s_ref[:, cs], MASK_VALUE)
                    s_ref[:, cs] = blk
                    mx_acc = jnp.maximum(mx_acc, blk)
                mx = jnp.maximum(jnp.max(mx_acc, axis=1, keepdims=True),
                                 jnp.max(s_own, axis=1, keepdims=True))
                p_own = jnp.exp(s_own - mx)
                l_acc = jnp.zeros((q_rows, MOBA_BLOCK), F32)
                for n in range(n_past_blk):
                    cs = slice(n * MOBA_BLOCK, (n + 1) * MOBA_BLOCK)
                    blk = jnp.exp(s_ref[:, cs] - mx)
                    s_ref[:, cs] = blk
                    l_acc = l_acc + blk
                s_ref[:, past_len:past_len + LANES] = p_own
                inv_l = 1.0 / (jnp.sum(l_acc, axis=1, keepdims=True)
                               + jnp.sum(p_own, axis=1, keepdims=True))
        else:
            cc = c - n_chunks
            p_c = s_ref[:, cc * chunk_keys:(cc + 1) * chunk_keys].astype(BF16)
            acc = acc + _dot_nt(p_c, buf[slot].astype(BF16))

    p_own = s_ref[:, past_len:past_len + LANES].astype(BF16)
    acc = (acc + _dot(p_own, pad_rows(vnew_ref[...]).astype(BF16))) * inv_l
    y = jnp.zeros((SAMPLE_ROWS, MOBA_DIM), F32)
    for t in range(n_q):
        grp = jnp.where(own_head, acc[t * MOBA_HEADS:(t + 1) * MOBA_HEADS, :], 0.0)
        y = jnp.where(sub_i == t, jnp.sum(grp, axis=0, keepdims=True), y)
    o_ref[...] = y


def _moba_sample(page_table, q, k_new, v_new, cache_k, cache_v, *, layer, n_q):
    n_req, n_pages = page_table.shape
    depth, n_pool, page_size = cache_k.shape[:3]
    ck = jnp.transpose(cache_k, (0, 1, 3, 4, 2)).reshape(depth * n_pool, MOBA_DIM, page_size)
    cv = jnp.transpose(cache_v, (0, 1, 3, 4, 2)).reshape(depth * n_pool, MOBA_DIM, page_size)
    chunk_keys = _PAGES_PER_CHUNK * page_size
    past_len = n_pages * page_size
    q_rows = n_q * MOBA_HEADS
    rows = pl.BlockSpec((SAMPLE_ROWS, MOBA_DIM), lambda b, pt: (b, 0))
    hbm = pl.BlockSpec(memory_space=pl.ANY)
    assert (2 * n_pages // _PAGES_PER_CHUNK) % _RING_SLOTS == 0
    scratch = [pltpu.VMEM((_RING_SLOTS, MOBA_DIM, chunk_keys), F32),
               pltpu.SemaphoreType.DMA((_RING_SLOTS,)),
               pltpu.VMEM((q_rows, past_len + LANES), F32)]
    blocks = (_nbytes((_RING_SLOTS, MOBA_DIM, chunk_keys), F32) + _nbytes((q_rows, past_len + LANES), F32)
              + 8 * _nbytes((SAMPLE_ROWS, MOBA_DIM), F32))
    return pl.pallas_call(
        functools.partial(_moba_sample_kernel, page_base=layer * n_pool, n_pages=n_pages,
                          page_size=page_size, n_q=n_q),
        grid_spec=pltpu.PrefetchScalarGridSpec(
            num_scalar_prefetch=1,
            grid=(n_req,),
            in_specs=[rows, rows, rows, hbm, hbm],
            out_specs=rows,
            scratch_shapes=scratch),
        out_shape=jax.ShapeDtypeStruct((n_req * SAMPLE_ROWS, MOBA_DIM), F32),
        compiler_params=pltpu.CompilerParams(
            dimension_semantics=("arbitrary",), vmem_limit_bytes=_vmem_limit(blocks)),
        name="moba_sample",
    )(page_table.reshape(-1), q, k_new, v_new, ck, cv)


def _sample_sgu_tables(sgu_w, sgu_b, n_b, t_new):
    t = jnp.arange(SAMPLE_ROWS)
    d = jnp.arange(t_new)
    src = t[None, :] - d[:, None]
    ok = (src >= 0) & (t[None, :] < t_new)
    w_td = sgu_w[:, t[None, :].clip(0, t_new - 1), src.clip(0, t_new - 1)]
    w_td = jnp.where(ok[None], w_td, 0.0)
    wd = jnp.repeat(jnp.transpose(w_td, (1, 2, 0)), SGU_GROUP_DIM, axis=2)
    bl = jnp.where((t < t_new)[:, None],
                   jnp.repeat(sgu_b[:, t.clip(0, t_new - 1)].T, SGU_GROUP_DIM, axis=1), 0.0)
    return jnp.tile(wd, (1, n_b, 1)), jnp.tile(bl, (n_b, 1))


def kernel(x_prompt, x_sample, mem_prompt, cache_k, cache_v, cache_mem_k, cache_mem_v, page_table,
           ffn1_norm, ffn1_w1, ffn1_w3, ffn1_w2, mix_norm, w_in, sgu_ln_g, sgu_ln_b, sgu_w, sgu_b,
           mem_norm, w_mem_kv, w_a_out, w_b_out, w_c_out, w_o, ffn2_norm, ffn2_w1, ffn2_w3, ffn2_w2,
           final_norm):
    depth = w_in.shape[0]
    bsz, t_len, _ = x_prompt.shape
    dec_b, t_new, _ = x_sample.shape
    n_pages = page_table.shape[1]
    page_size = cache_k.shape[2]
    past_len = n_pages * page_size
    mem_len = mem_prompt.shape[1]
    assert t_len % 512 == 0 and t_new <= SAMPLE_ROWS and past_len % MOBA_BLOCK == 0
    assert n_pages % _PAGES_PER_CHUNK == 0 and (_PAGES_PER_CHUNK * page_size) % MOBA_BLOCK == 0

    row = lambda a: a.reshape(1, -1)
    fin = row(final_norm)
    f1 = tuple(w.astype(BF16) for w in (ffn1_w1, ffn1_w3, ffn1_w2))
    f2 = tuple(w.astype(BF16) for w in (ffn2_w1, ffn2_w3, ffn2_w2))
    win, wkv = w_in.astype(BF16), w_mem_kv.astype(BF16)
    outs_w = tuple(w.astype(BF16) for w in (w_a_out, w_b_out, w_c_out, w_o))

    tm_p = 512
    cos_p, slo_p, shi_p = _rope_tables(jnp.arange(t_len, dtype=jnp.int32))
    h = x_prompt.reshape(bsz * t_len, D_MODEL)
    mem = mem_prompt.reshape(bsz * mem_len, D_MODEL)
    kp, vp, mkp, mvp = [], [], [], []
    for l in range(depth):
        mk, mv = _memkv(mem, row(mem_norm[l]), wkv, layer=l, tm=512)
        h = _ffn(h, row(ffn1_norm[l]), *f1, fin, layer=l, tm=tm_p, final_norm=False)
        u, vn, q, k, v, cq, gate, k_t, v_t = _inproj(
            h, row(mix_norm[l]), win, row(sgu_ln_g[l]), row(sgu_ln_b[l]), cos_p, slo_p, shi_p,
            layer=l, tm=tm_p, t_major_kv=True, act_dtype=BF16)
        yb = _moba_prompt(q.reshape(bsz, t_len, MOBA_DIM), k.reshape(bsz, t_len, MOBA_DIM),
                          v.reshape(bsz, t_len, MOBA_DIM), out_dtype=BF16).reshape(bsz * t_len, MOBA_DIM)
        h = _mix_prompt(u, vn, cq, gate, yb, h, mk.reshape(bsz, mem_len, MEM_DIM),
                        mv.reshape(bsz, mem_len, MEM_DIM), sgu_w, sgu_b[l].T, *outs_w,
                        layer=l, tm=tm_p, t_len=t_len)
        h = _ffn(h, row(ffn2_norm[l]), *f2, fin, layer=l, tm=tm_p, final_norm=(l == depth - 1))
        kp.append(k_t)
        vp.append(v_t)
        mkp.append(mk.reshape(bsz, mem_len, MEM_HEADS, MEM_HEAD_DIM))
        mvp.append(mv.reshape(bsz, mem_len, MEM_HEADS, MEM_HEAD_DIM))
    y_prompt = h.reshape(bsz, t_len, D_MODEL)

    def token_minor_to_rows(ts):
        stacked = jnp.stack(ts).reshape(depth, bsz, MOBA_HEADS, MOBA_HEAD_DIM, t_len)
        return jnp.transpose(stacked, (0, 1, 4, 2, 3))

    m_s = dec_b * SAMPLE_ROWS
    pos_s = past_len + jnp.arange(SAMPLE_ROWS, dtype=jnp.int32)
    cos_s, slo_s, shi_s = (jnp.tile(t, (dec_b, 1)) for t in _rope_tables(pos_s))
    h = jnp.pad(x_sample, ((0, 0), (0, SAMPLE_ROWS - t_new), (0, 0))).reshape(m_s, D_MODEL)
    n_b = 8
    mem_k = cache_mem_k.reshape(depth, dec_b, mem_len * MEM_HEADS, MEM_HEAD_DIM)
    mem_v = cache_mem_v.reshape(depth, dec_b, mem_len * MEM_HEADS, MEM_HEAD_DIM)
    ks_, vs_, sv_ = [], [], []
    unpad = lambda a: a.reshape(dec_b, SAMPLE_ROWS, -1)[:, :t_new]
    for l in range(depth):
        wd, bl = _sample_sgu_tables(sgu_w[l], sgu_b[l], n_b, t_new)
        h = _ffn(h, row(ffn1_norm[l]), *f1, fin, layer=l, tm=m_s, final_norm=False)
        u, vn, q, k, v, cq, gate = _inproj(
            h, row(mix_norm[l]), win, row(sgu_ln_g[l]), row(sgu_ln_b[l]), cos_s, slo_s, shi_s,
            layer=l, tm=m_s, t_major_kv=False, act_dtype=F32)
        ya, yc = _premix_sample(u, vn, cq, mem_k, mem_v, wd, bl, layer=l, n_b=n_b)
        yb = _moba_sample(page_table, q, k, v, cache_k, cache_v, layer=l, n_q=t_new)
        h = _merge_sample(h, ya, yb, yc, gate, *outs_w, layer=l)
        h = _ffn(h, row(ffn2_norm[l]), *f2, fin, layer=l, tm=m_s, final_norm=(l == depth - 1))
        ks_.append(unpad(k).reshape(dec_b, t_new, MOBA_HEADS, MOBA_HEAD_DIM))
        vs_.append(unpad(v).reshape(dec_b, t_new, MOBA_HEADS, MOBA_HEAD_DIM))
        sv_.append(unpad(vn))
    y_sample = unpad(h)

    return (y_prompt, y_sample, token_minor_to_rows(kp), token_minor_to_rows(vp),
            jnp.stack(mkp), jnp.stack(mvp), jnp.stack(ks_), jnp.stack(vs_), jnp.stack(sv_))
```

```python
import functools

import jax
import jax.numpy as jnp
import numpy as np
from jax import lax
from jax.experimental import pallas as pl
from jax.experimental.pallas import tpu as pltpu

F32 = jnp.float32
BF16 = jnp.bfloat16

D_MODEL = 1024
SGU_GROUPS = 4
SGU_GROUP_DIM = 128
SGU_DIM = 512
CHUNK = 128
MOBA_HEADS = 8
MOBA_HEAD_DIM = 64
MOBA_DIM = 512
MOBA_BLOCK = 256
MOBA_TOPK = 3
MEM_HEADS = 4
MEM_HEAD_DIM = 128
MEM_DIM = 512
D_FF = 2816
ROPE_THETA = 10000.0
NORM_EPS = 1e-6
MASK_VALUE = -1e30
LOG2_E = float(np.log2(np.e))
IN_COLS = 6144
GATE_COL0 = 3072

V7X_VMEM_BYTES = 64 * 1024 * 1024
LANES = 128
SUBLANES = 8
SAMPLE_ROWS = 8
TEMP_ALLOWANCE = 12 * 1024 * 1024

NT_DIMS = (((1,), (1,)), ((), ()))


def _vmem_limit(block_bytes):
    return int(min(block_bytes + TEMP_ALLOWANCE, V7X_VMEM_BYTES - 4 * 1024 * 1024))


def _nbytes(shape, dtype):
    return int(np.prod(shape)) * jnp.dtype(dtype).itemsize


def _resident(shape):
    nd = len(shape)
    return pl.BlockSpec(shape, lambda *_: (0,) * nd, pipeline_mode=pl.Buffered(1))


def _layer_resident(arr, layer):
    nd = arr.ndim
    return pl.BlockSpec((None,) + arr.shape[1:], lambda *_: (layer,) + (0,) * (nd - 1),
                        pipeline_mode=pl.Buffered(1))


def _dot(a, b):
    return jnp.dot(a, b, preferred_element_type=F32)


def _dot_nt(a, b, precision=None):
    return lax.dot_general(a, b, NT_DIMS, precision=precision, preferred_element_type=F32)


def _rms(x, g):
    return x * lax.rsqrt(jnp.mean(x * x, axis=-1, keepdims=True) + NORM_EPS) * g


def _sigmoid(x):
    return 1.0 / (1.0 + jnp.exp(-x))


def _gelu(x):
    return 0.5 * x * (1.0 + lax.erf(x * np.float32(np.sqrt(0.5))))


def _softmax_rows(s):
    p = jnp.exp(s - jnp.max(s, axis=-1, keepdims=True))
    return p, 1.0 / jnp.sum(p, axis=-1, keepdims=True)


_FF_CHUNK = 512
_FF_CHUNKS = tuple((s, min(_FF_CHUNK, D_FF - s)) for s in range(0, D_FF, _FF_CHUNK))


def _ffn_kernel(x_ref, g_ref, w1_ref, w3_ref, w2_ref, fg_ref, o_ref, a_ref, *, final_norm):
    x = x_ref[...]
    xb = _rms(x, g_ref[...]).astype(BF16)
    for s, w in _FF_CHUNKS:
        h1 = _dot(xb, w1_ref[:, s:s + w])
        h3 = _dot(xb, w3_ref[:, s:s + w])
        a_ref[:, s:s + w] = (h1 * _sigmoid(h1) * h3).astype(BF16)
    y = x + 0.5 * _dot(a_ref[...], w2_ref[...])
    if final_norm:
        y = _rms(y, fg_ref[...])
    o_ref[...] = y


def _ffn(x, g, w1, w3, w2, fg, *, layer, tm, final_norm):
    m = x.shape[0]
    blocks = (2 * 2 * _nbytes((tm, D_MODEL), F32) + 3 * _nbytes((D_MODEL, D_FF), BF16)
              + _nbytes((tm, D_FF), BF16))
    return pl.pallas_call(
        functools.partial(_ffn_kernel, final_norm=final_norm),
        grid=(m // tm,),
        in_specs=[
            pl.BlockSpec((tm, D_MODEL), lambda i: (i, 0)),
            _resident((1, D_MODEL)),
            _layer_resident(w1, layer),
            _layer_resident(w3, layer),
            _layer_resident(w2, layer),
            _resident((1, D_MODEL)),
        ],
        out_specs=pl.BlockSpec((tm, D_MODEL), lambda i: (i, 0)),
        out_shape=jax.ShapeDtypeStruct((m, D_MODEL), F32),
        scratch_shapes=[pltpu.VMEM((tm, D_FF), BF16)],
        compiler_params=pltpu.CompilerParams(
            dimension_semantics=("parallel",), vmem_limit_bytes=_vmem_limit(blocks)),
        name="ffn",
    )(x, g, w1, w3, w2, fg)


def _rope(h, cos, sin_lo, sin_hi):
    outs = []
    for c in range(h.shape[1] // LANES):
        s = h[:, c * LANES:(c + 1) * LANES]
        outs.append(s * cos + pltpu.roll(s, LANES - 32, 1) * sin_lo + pltpu.roll(s, 32, 1) * sin_hi)
    return jnp.concatenate(outs, axis=1)


def _inproj_kernel(x_ref, g_ref, w_ref, lng_ref, lnb_ref, cos_ref, slo_ref, shi_ref,
                   u_ref, vn_ref, q_ref, k_ref, v_ref, cq_ref, gate_ref, *t_refs):
    xb = _rms(x_ref[...], g_ref[...]).astype(BF16)

    def proj(c0, w=512):
        return _dot(xb, w_ref[:, c0:c0 + w])

    u_ref[...] = _gelu(proj(0)).astype(u_ref.dtype)
    a = _gelu(proj(512))
    ac = a - jnp.mean(a, axis=-1, keepdims=True)
    vn_ref[...] = (ac * lax.rsqrt(jnp.mean(ac * ac, axis=-1, keepdims=True) + NORM_EPS)
                   * lng_ref[...] + lnb_ref[...]).astype(vn_ref.dtype)
    cos, slo, shi = cos_ref[...], slo_ref[...], shi_ref[...]
    q_ref[...] = _rope(proj(1024), cos, slo, shi)
    k = _rope(proj(1536), cos, slo, shi)
    v = proj(2048)
    k_ref[...] = k
    v_ref[...] = v
    if t_refs:
        t_refs[0][...] = k.T
        t_refs[1][...] = v.T
    cq_ref[...] = proj(2560).astype(cq_ref.dtype)
    for c in range(6):
        gate_ref[:, c * 512:(c + 1) * 512] = _sigmoid(proj(GATE_COL0 + c * 512)).astype(gate_ref.dtype)


def _inproj(x, g, w, lng, lnb, cos, slo, shi, *, layer, tm, t_major_kv, act_dtype):
    m = x.shape[0]
    t_len = cos.shape[0]
    n_tab = t_len // tm
    out_cols = (512,) * 6 + (3 * D_MODEL,)
    blocks = (2 * _nbytes((tm, D_MODEL), F32) + _nbytes((D_MODEL, IN_COLS), BF16)
              + 2 * _nbytes((tm, IN_COLS), F32) + 6 * _nbytes((tm, LANES), F32))
    tab = pl.BlockSpec((tm, LANES), lambda i: (i % n_tab, 0))
    out_specs = [pl.BlockSpec((tm, c), lambda i: (i, 0)) for c in out_cols]
    out_dtypes = (act_dtype, act_dtype, F32, F32, F32, act_dtype, act_dtype)
    out_shape = [jax.ShapeDtypeStruct((m, c), dt) for c, dt in zip(out_cols, out_dtypes)]
    if t_major_kv:
        blocks += 4 * _nbytes((MOBA_DIM, tm), F32)
        out_specs += [pl.BlockSpec((None, MOBA_DIM, tm), lambda i: (i // n_tab, 0, i % n_tab))] * 2
        out_shape += [jax.ShapeDtypeStruct((m // t_len, MOBA_DIM, t_len), F32)] * 2
    return pl.pallas_call(
        _inproj_kernel,
        grid=(m // tm,),
        in_specs=[
            pl.BlockSpec((tm, D_MODEL), lambda i: (i, 0)),
            _resident((1, D_MODEL)),
            _layer_resident(w, layer),
            _resident((1, SGU_DIM)),
            _resident((1, SGU_DIM)),
            tab, tab, tab,
        ],
        out_specs=out_specs,
        out_shape=out_shape,
        compiler_params=pltpu.CompilerParams(
            dimension_semantics=("parallel",), vmem_limit_bytes=_vmem_limit(blocks)),
        name="inproj",
    )(x, g, w, lng, lnb, cos, slo, shi)


def _rope_tables(pos):
    half = MOBA_HEAD_DIM // 2
    inv = ROPE_THETA ** (-jnp.arange(half, dtype=F32) * 2.0 / MOBA_HEAD_DIM)
    ang = pos.astype(F32)[:, None] * inv[None, :]
    cos, sin = jnp.tile(jnp.cos(ang), (1, 4)), jnp.tile(jnp.sin(ang), (1, 4))
    first_half = (jnp.arange(LANES) % MOBA_HEAD_DIM) < half
    return cos, jnp.where(first_half, -sin, 0.0), jnp.where(first_half, 0.0, sin)


def _memkv_kernel(m_ref, g_ref, w_ref, k_ref, v_ref):
    xb = _rms(m_ref[...], g_ref[...]).astype(BF16)
    k_ref[...] = _dot(xb, w_ref[:, :MEM_DIM])
    v_ref[...] = _dot(xb, w_ref[:, MEM_DIM:])


def _memkv(mem, g, w, *, layer, tm):
    m = mem.shape[0]
    blocks = (2 * _nbytes((tm, D_MODEL), F32) + _nbytes((D_MODEL, 2 * MEM_DIM), BF16)
              + 4 * _nbytes((tm, MEM_DIM), F32))
    return pl.pallas_call(
        _memkv_kernel,
        grid=(m // tm,),
        in_specs=[pl.BlockSpec((tm, D_MODEL), lambda i: (i, 0)), _resident((1, D_MODEL)),
                  _layer_resident(w, layer)],
        out_specs=[pl.BlockSpec((tm, MEM_DIM), lambda i: (i, 0))] * 2,
        out_shape=[jax.ShapeDtypeStruct((m, MEM_DIM), F32)] * 2,
        compiler_params=pltpu.CompilerParams(
            dimension_semantics=("parallel",), vmem_limit_bytes=_vmem_limit(blocks)),
        name="memkv",
    )(mem, g, w)


def _topk_select(g, n_past, n_rows):
    row = lax.broadcasted_iota(jnp.int32, g.shape, 0)
    rank = jnp.zeros(g.shape, F32)
    for m in range(n_past):
        gm = g[m:m + 1, :]
        beats = (gm > g) | ((gm == g) & (row > m))
        rank = rank + beats.astype(F32)
    return (row < n_past) & (rank < MOBA_TOPK)


_SCORES_AHEAD = 1


def _moba_prompt_kernel(q_ref, k_ref, v_ref, o_ref):
    t_len = q_ref.shape[1]
    n_blk = t_len // MOBA_BLOCK
    k = k_ref[0]
    kb = k.astype(BF16)
    v_aug = jnp.concatenate([v_ref[0].astype(BF16), jnp.ones((t_len, LANES), BF16)], axis=1)
    k_mean = jnp.sum(k.reshape(n_blk, MOBA_BLOCK, LANES), axis=1) * (1.0 / MOBA_BLOCK)
    lane = lax.broadcasted_iota(jnp.int32, (1, LANES), 1)
    head_masks = (lane < MOBA_HEAD_DIM, lane >= MOBA_HEAD_DIM)
    r_i = lax.broadcasted_iota(jnp.int32, (2 * MOBA_BLOCK, MOBA_BLOCK), 0) & (MOBA_BLOCK - 1)
    c_i = lax.broadcasted_iota(jnp.int32, (2 * MOBA_BLOCK, MOBA_BLOCK), 1)
    causal2 = c_i <= r_i
    scale = MOBA_HEAD_DIM ** -0.5 * LOG2_E

    def scores(qi):
        q_blk = q_ref[0, qi * MOBA_BLOCK:(qi + 1) * MOBA_BLOCK, :]
        q2 = jnp.concatenate([jnp.where(hm, q_blk, 0.0) for hm in head_masks], axis=0)
        return q2, _dot_nt((q2 * scale).astype(BF16), kb[:(qi + 1) * MOBA_BLOCK])

    def weighted_values(qi, p):
        o2 = _dot(p, v_aug[:(qi + 1) * MOBA_BLOCK])
        o2 = o2[:, :LANES] * (1.0 / o2[:, LANES:])
        o_ref[0, qi * MOBA_BLOCK:(qi + 1) * MOBA_BLOCK, :] = jnp.where(
            head_masks[0], o2[:MOBA_BLOCK], o2[MOBA_BLOCK:]).astype(o_ref.dtype)

    ahead = [scores(qi) for qi in range(min(_SCORES_AHEAD, n_blk))]
    pending = None
    for qi in range(n_blk):
        t_k = (qi + 1) * MOBA_BLOCK
        q2, s = ahead.pop(0)
        if qi + _SCORES_AHEAD < n_blk:
            ahead.append(scores(qi + _SCORES_AHEAD))
        parts = []
        if qi > MOBA_TOPK:
            gate = _dot_nt(k_mean, q2, precision=lax.Precision.HIGHEST)
            sel = _topk_select(gate, qi, n_blk).astype(F32)
            sel_t = jnp.concatenate(
                [sel, jnp.zeros((LANES - n_blk, 2 * MOBA_BLOCK), F32)], axis=0).T
            for n in range(qi):
                keep = sel_t[:, n:n + 1] > 0.5
                parts.append(jnp.where(keep, s[:, n * MOBA_BLOCK:(n + 1) * MOBA_BLOCK], MASK_VALUE))
        else:
            for n in range(qi):
                parts.append(s[:, n * MOBA_BLOCK:(n + 1) * MOBA_BLOCK])
        parts.append(jnp.where(causal2, s[:, qi * MOBA_BLOCK:t_k], MASK_VALUE))
        sm = jnp.concatenate(parts, axis=1) if len(parts) > 1 else parts[0]
        p = jnp.exp2(sm - jnp.max(sm, axis=-1, keepdims=True)).astype(BF16)
        if pending is not None:
            weighted_values(*pending)
        pending = (qi, p)
    weighted_values(*pending)


def _moba_prompt(q, k, v, *, out_dtype):
    bsz, t_len, _ = q.shape
    spec = pl.BlockSpec((1, t_len, LANES), lambda b, h: (b, 0, h))
    blocks = (8 * _nbytes((t_len, LANES), F32) + 2 * _nbytes((t_len, LANES), BF16)
              + 8 * _nbytes((MOBA_BLOCK, t_len), F32))
    return pl.pallas_call(
        _moba_prompt_kernel,
        grid=(bsz, MOBA_DIM // LANES),
        in_specs=[spec, spec, spec],
        out_specs=spec,
        out_shape=jax.ShapeDtypeStruct((bsz, t_len, MOBA_DIM), out_dtype),
        compiler_params=pltpu.CompilerParams(
            dimension_semantics=("parallel", "parallel"), vmem_limit_bytes=_vmem_limit(blocks)),
        name="moba_prompt",
    )(q, k, v)


def _merge(x, ya, yb, yc, gate_ref, wa_ref, wb_ref, wc_ref, wo_ref):
    t = (gate_ref[:, 0:D_MODEL] * _dot(ya.astype(BF16), wa_ref[...])
         + gate_ref[:, D_MODEL:2 * D_MODEL] * _dot(yb.astype(BF16), wb_ref[...])
         + gate_ref[:, 2 * D_MODEL:3 * D_MODEL] * _dot(yc.astype(BF16), wc_ref[...]))
    return x + _dot(t.astype(BF16), wo_ref[...])


def _mem_attention(cq, mkb, mvb):
    scale = MEM_HEAD_DIM ** -0.5
    outs = []
    for h in range(MEM_HEADS):
        sl = slice(h * MEM_HEAD_DIM, (h + 1) * MEM_HEAD_DIM)
        s = _dot_nt(cq[:, sl].astype(BF16), mkb[:, sl]) * scale
        p, inv_l = _softmax_rows(s)
        outs.append(_dot(p.astype(BF16), mvb[:, sl]) * inv_l)
    return jnp.concatenate(outs, axis=1)


def _mix_prompt_kernel(u_ref, vn_ref, cq_ref, gate_ref, yb_ref, x_ref, mk_ref, mv_ref,
                       sw_ref, sb_ref, wa_ref, wb_ref, wc_ref, wo_ref, o_ref):
    tm = u_ref.shape[0]
    r_i = lax.broadcasted_iota(jnp.int32, (CHUNK, CHUNK), 0)
    c_i = lax.broadcasted_iota(jnp.int32, (CHUNK, CHUNK), 1)
    wm = [jnp.where(c_i <= r_i, sw_ref[g], 0.0).astype(BF16) for g in range(SGU_GROUPS)]
    rows = []
    for ch in range(tm // CHUNK):
        rs = slice(ch * CHUNK, (ch + 1) * CHUNK)
        cols = []
        for g in range(SGU_GROUPS):
            cs = slice(g * SGU_GROUP_DIM, (g + 1) * SGU_GROUP_DIM)
            y = _dot(wm[g], vn_ref[rs, cs].astype(BF16)) + sb_ref[:, g:g + 1]
            cols.append(u_ref[rs, cs].astype(F32) * y)
        rows.append(jnp.concatenate(cols, axis=1))
    ya = jnp.concatenate(rows, axis=0)
    yc = _mem_attention(cq_ref[...], mk_ref[0].astype(BF16), mv_ref[0].astype(BF16))
    o_ref[...] = _merge(x_ref[...], ya, yb_ref[...], yc, gate_ref, wa_ref, wb_ref, wc_ref, wo_ref)


def _mix_prompt(u, vn, cq, gate, yb, x, mk, mv, sw, sb_t, wa, wb, wc, wo, *, layer, tm, t_len):
    m = x.shape[0]
    per_b = t_len // tm

    def rows(c):
        return pl.BlockSpec((tm, c), lambda i: (i, 0))

    mem_spec = pl.BlockSpec((1,) + mk.shape[1:], lambda i: (i // per_b, 0, 0))
    blocks = (2 * _nbytes((tm, 4 * 512 + 3 * D_MODEL + 2 * D_MODEL), F32)
              + 4 * _nbytes(mk.shape[1:], F32) + _nbytes((3 * 512 + D_MODEL, D_MODEL), BF16))
    return pl.pallas_call(
        _mix_prompt_kernel,
        grid=(m // tm,),
        in_specs=[rows(512), rows(512), rows(512), rows(3 * D_MODEL), rows(512), rows(D_MODEL),
                  mem_spec, mem_spec,
                  _layer_resident(sw, layer), _resident(sb_t.shape),
                  _layer_resident(wa, layer), _layer_resident(wb, layer),
                  _layer_resident(wc, layer), _layer_resident(wo, layer)],
        out_specs=rows(D_MODEL),
        out_shape=jax.ShapeDtypeStruct((m, D_MODEL), F32),
        compiler_params=pltpu.CompilerParams(
            dimension_semantics=("parallel",), vmem_limit_bytes=_vmem_limit(blocks)),
        name="mix_prompt",
    )(u, vn, cq, gate, yb, x, mk, mv, sw, sb_t, wa, wb, wc, wo)


def _premix_sample_kernel(u_ref, vn_ref, cq_ref, mk_ref, mv_ref, wd_ref, bl_ref, ya_ref, yc_ref):
    vn = vn_ref[...]
    y = bl_ref[...] + wd_ref[0] * vn
    for d in range(1, wd_ref.shape[0]):
        y = y + wd_ref[d] * pltpu.roll(vn, d, 0)
    ya_ref[...] = u_ref[...] * y

    q_rows = MEM_HEADS * SAMPLE_ROWS
    n_keys = mk_ref.shape[1]
    row_head = lax.broadcasted_iota(jnp.int32, (q_rows, n_keys), 0) // SAMPLE_ROWS
    key_head = lax.broadcasted_iota(jnp.int32, (q_rows, n_keys), 1) % MEM_HEADS
    same_head = row_head == key_head
    scale = MEM_HEAD_DIM ** -0.5
    for b in range(mk_ref.shape[0]):
        rs = slice(b * SAMPLE_ROWS, (b + 1) * SAMPLE_ROWS)
        q8 = cq_ref[rs, :]
        q_h = jnp.concatenate([q8[:, h * MEM_HEAD_DIM:(h + 1) * MEM_HEAD_DIM]
                               for h in range(MEM_HEADS)], axis=0)
        s = _dot_nt(q_h.astype(BF16), mk_ref[b].astype(BF16)) * scale
        p, inv_l = _softmax_rows(jnp.where(same_head, s, MASK_VALUE))
        o = _dot(p.astype(BF16), mv_ref[b].astype(BF16)) * inv_l
        yc_ref[rs, :] = jnp.concatenate([o[h * SAMPLE_ROWS:(h + 1) * SAMPLE_ROWS, :]
                                         for h in range(MEM_HEADS)], axis=1)


def _premix_sample(u, vn, cq, mk, mv, wd, bl, *, layer, n_b):
    m = u.shape[0]
    tm = n_b * SAMPLE_ROWS
    rows = pl.BlockSpec((tm, 512), lambda i: (i, 0))
    mem_block = (n_b,) + mk.shape[2:]
    mem_spec = pl.BlockSpec((None,) + mem_block, lambda i: (layer, i, 0, 0))
    blocks = (2 * 5 * _nbytes((tm, 512), F32) + 4 * _nbytes(mem_block, F32)
              + _nbytes(wd.shape, F32) + _nbytes(bl.shape, F32))
    return pl.pallas_call(
        _premix_sample_kernel,
        grid=(m // tm,),
        in_specs=[rows, rows, rows, mem_spec, mem_spec, _resident(wd.shape), _resident(bl.shape)],
        out_specs=[rows, rows],
        out_shape=[jax.ShapeDtypeStruct((m, 512), F32)] * 2,
        compiler_params=pltpu.CompilerParams(
            dimension_semantics=("parallel",), vmem_limit_bytes=_vmem_limit(blocks)),
        name="premix_sample",
    )(u, vn, cq, mk, mv, wd, bl)


def _merge_kernel(x_ref, ya_ref, yb_ref, yc_ref, gate_ref, wa_ref, wb_ref, wc_ref, wo_ref, o_ref):
    o_ref[...] = _merge(x_ref[...], ya_ref[...], yb_ref[...], yc_ref[...], gate_ref,
                        wa_ref, wb_ref, wc_ref, wo_ref)


def _merge_sample(x, ya, yb, yc, gate, wa, wb, wc, wo, *, layer):
    m = x.shape[0]
    acts, weights = (x, ya, yb, yc, gate), (wa, wb, wc, wo)
    args = acts + weights
    blocks = (sum(_nbytes(a.shape, a.dtype) for a in acts) + _nbytes(x.shape, F32)
              + sum(_nbytes(w.shape[1:], w.dtype) for w in weights))
    return pl.pallas_call(
        _merge_kernel,
        grid=(1,),
        in_specs=[_resident(a.shape) for a in acts] + [_layer_resident(w, layer) for w in weights],
        out_specs=pl.BlockSpec((m, D_MODEL), lambda i: (0, 0)),
        out_shape=jax.ShapeDtypeStruct((m, D_MODEL), F32),
        compiler_params=pltpu.CompilerParams(
            dimension_semantics=("arbitrary",), vmem_limit_bytes=_vmem_limit(blocks)),
        name="merge_sample",
    )(*args)


_PAGES_PER_CHUNK = 8
_RING_SLOTS = 4
_RING_AHEAD = _RING_SLOTS - 1


def _moba_sample_steps(pt_ref, q_ref, kn_ref, vnew_ref, ck_hbm, cv_hbm, o_ref,
                       buf, sem, s_ref, *, page_base, n_pages, page_size, n_q):
    b = pl.program_id(0)
    n_req = pl.num_programs(0)
    chunk_keys = _PAGES_PER_CHUNK * page_size
    n_chunks = n_pages // _PAGES_PER_CHUNK
    blocks_per_chunk = chunk_keys // MOBA_BLOCK
    n_past_blk = n_chunks * blocks_per_chunk
    past_len = n_pages * page_size
    q_rows = n_q * MOBA_HEADS

    def copies(req, c, slot):
        src = ck_hbm if c < n_chunks else cv_hbm
        first = req * n_pages + (c % n_chunks) * _PAGES_PER_CHUNK
        return [pltpu.make_async_copy(src.at[pt_ref[first + p] + page_base],
                                      buf.at[slot, :, pl.ds(p * page_size, page_size)],
                                      sem.at[slot])
                for p in range(_PAGES_PER_CHUNK)]

    def start(req, c, slot):
        for cp in copies(req, c, slot):
            cp.start()

    def wait(req, c, slot):
        for cp in copies(req, c, slot):
            cp.wait()

    n_total = 2 * n_chunks

    @pl.when(b == 0)
    def _():
        for c0 in range(_RING_AHEAD):
            start(b, c0, c0 % _RING_SLOTS)

    sub_i = lax.broadcasted_iota(jnp.int32, (SAMPLE_ROWS, MOBA_DIM), 0)
    own_head = sub_i == lax.broadcasted_iota(jnp.int32, (SAMPLE_ROWS, MOBA_DIM), 1) // MOBA_HEAD_DIM
    q8 = q_ref[...] * (MOBA_HEAD_DIM ** -0.5)
    q_exp = jnp.concatenate(
        [jnp.where(own_head, jnp.broadcast_to(q8[t:t + 1, :], (MOBA_HEADS, MOBA_DIM)), 0.0)
         for t in range(n_q)], axis=0)
    q_exp_b = q_exp.astype(BF16)

    def pad_rows(x8):
        return jnp.concatenate([x8, jnp.zeros((LANES - SAMPLE_ROWS, x8.shape[1]), F32)], axis=0)

    blk_lane = lax.broadcasted_iota(jnp.int32, (MOBA_DIM, LANES), 1)
    ksum = jnp.zeros((MOBA_DIM, LANES), F32)
    acc = jnp.zeros((q_rows, MOBA_DIM), F32)
    inv_l = None
    for c in range(n_total):
        slot = c % _RING_SLOTS
        wait(b, c, slot)
        nxt = c + _RING_AHEAD
        if nxt < n_total:
            start(b, nxt, nxt % _RING_SLOTS)
        else:
            @pl.when(b + 1 < n_req)
            def _(nxt=nxt):
                start(b + 1, nxt - n_total, nxt % _RING_SLOTS)

        if c < n_chunks:
            kc = buf[slot]
            for j in range(blocks_per_chunk):
                col = jnp.sum(kc[:, j * MOBA_BLOCK:(j + 1) * MOBA_BLOCK], axis=1, keepdims=True)
                ksum = jnp.where(blk_lane == c * blocks_per_chunk + j, col, ksum)
            s_ref[:, c * chunk_keys:(c + 1) * chunk_keys] = _dot(q_exp_b, kc.astype(BF16))
            if c == n_chunks - 1:
                gate = jnp.dot(q_exp, ksum * (1.0 / MOBA_BLOCK), precision=lax.Precision.HIGHEST,
                               preferred_element_type=F32)
                lane = lax.broadcasted_iota(jnp.int32, gate.shape, 1)
                rank = jnp.zeros(gate.shape, F32)
                for m in range(n_past_blk):
                    gm = gate[:, m:m + 1]
                    beats = (gm > gate) | ((gm == gate) & (lane > m))
                    rank = rank + beats.astype(F32)
                sel = ((lane < n_past_blk) & (rank < MOBA_TOPK)).astype(F32)
                s_own = _dot_nt(q_exp_b, pad_rows(kn_ref[...]).astype(BF16))
                t_i = lax.broadcasted_iota(jnp.int32, s_own.shape, 0) // MOBA_HEADS
                s_own = jnp.where(lane <= t_i, s_own, MASK_VALUE)
                mx_acc = jnp.full((q_rows, MOBA_BLOCK), MASK_VALUE, F32)
                for n in range(n_past_blk):
                    cs = slice(n * MOBA_BLOCK, (n + 1) * MOBA_BLOCK)
                    blk = jnp.where(sel[:, n:n + 1] > 0.5, s_ref[:, cs], MASK_VALUE)
                    s_ref[:, cs] = blk
                    mx_acc = jnp.maximum(mx_acc, blk)
                mx = jnp.maximum(jnp.max(mx_acc, axis=1, keepdims=True),
                                 jnp.max(s_own, axis=1, keepdims=True))
                p_own = jnp.exp(s_own - mx)
                l_acc = jnp.zeros((q_rows, MOBA_BLOCK), F32)
                for n in range(n_past_blk):
                    cs = slice(n * MOBA_BLOCK, (n + 1) * MOBA_BLOCK)
                    blk = jnp.exp(s_ref[:, cs] - mx)
                    s_ref[:, cs] = blk
                    l_acc = l_acc + blk
                s_ref[:, past_len:past_len + LANES] = p_own
                inv_l = 1.0 / (jnp.sum(l_acc, axis=1, keepdims=True)
                               + jnp.sum(p_own, axis=1, keepdims=True))
        else:
            cc = c - n_chunks
            p_c = s_ref[:, cc * chunk_keys:(cc + 1) * chunk_keys].astype(BF16)
            acc = acc + _dot_nt(p_c, buf[slot].astype(BF16))
        yield c

    p_own = s_ref[:, past_len:past_len + LANES].astype(BF16)
    acc = (acc + _dot(p_own, pad_rows(vnew_ref[...]).astype(BF16))) * inv_l
    y = jnp.zeros((SAMPLE_ROWS, MOBA_DIM), F32)
    for t in range(n_q):
        grp = jnp.where(own_head, acc[t * MOBA_HEADS:(t + 1) * MOBA_HEADS, :], 0.0)
        y = jnp.where(sub_i == t, jnp.sum(grp, axis=0, keepdims=True), y)
    o_ref[...] = y


def _ffn_steps(x_ref, g_ref, w1_ref, w3_ref, w2_ref, o_ref, a_ref):
    x = x_ref[...]
    xb = _rms(x, g_ref[...]).astype(BF16)

    def hidden(s, w):
        h1 = _dot(xb, w1_ref[:, s:s + w])
        h3 = _dot(xb, w3_ref[:, s:s + w])
        a_ref[:, s:s + w] = (h1 * _sigmoid(h1) * h3).astype(BF16)

    def out_cols(c0, c1):
        o_ref[:, c0:c1] = x[:, c0:c1] + 0.5 * _dot(a_ref[...], w2_ref[:, c0:c1])

    half = D_MODEL // 2
    return ([functools.partial(hidden, s, w) for s, w in _FF_CHUNKS]
            + [functools.partial(out_cols, 0, half), functools.partial(out_cols, half, D_MODEL)])


def _ffn_moba_kernel(pt_ref, x_ref, g_ref, w1_ref, w3_ref, w2_ref, q_ref, kn_ref, vnew_ref,
                     ck_hbm, cv_hbm, o_ref, yb_ref, a_ref, buf, sem, s_ref, **moba_params):
    ffn = _ffn_steps(x_ref, g_ref, w1_ref, w3_ref, w2_ref, o_ref, a_ref)
    moba = _moba_sample_steps(pt_ref, q_ref, kn_ref, vnew_ref, ck_hbm, cv_hbm, yb_ref,
                              buf, sem, s_ref, **moba_params)
    n_chunk_steps = 2 * moba_params["n_pages"] // _PAGES_PER_CHUNK
    every = max(1, n_chunk_steps // len(ffn))
    for c in moba:
        if (c + 1) % every == 0 and ffn:
            ffn.pop(0)()
    for piece in ffn:
        piece()


def _ffn_moba(x, g, w1, w3, w2, page_table, q, k_new, v_new, cache_k, cache_v, *, layer, tm, n_q):
    m = x.shape[0]
    n_req, n_pages = page_table.shape
    assert m // tm == n_req
    depth, n_pool, page_size = cache_k.shape[:3]
    ck = jnp.transpose(cache_k, (0, 1, 3, 4, 2)).reshape(depth * n_pool, MOBA_DIM, page_size)
    cv = jnp.transpose(cache_v, (0, 1, 3, 4, 2)).reshape(depth * n_pool, MOBA_DIM, page_size)
    chunk_keys = _PAGES_PER_CHUNK * page_size
    past_len = n_pages * page_size
    q_rows = n_q * MOBA_HEADS
    assert (2 * n_pages // _PAGES_PER_CHUNK) % _RING_SLOTS == 0
    tile = pl.BlockSpec((tm, D_MODEL), lambda i, pt: (i, 0))
    rows = pl.BlockSpec((SAMPLE_ROWS, MOBA_DIM), lambda i, pt: (i, 0))
    hbm = pl.BlockSpec(memory_space=pl.ANY)
    scratch = [pltpu.VMEM((tm, D_FF), BF16),
               pltpu.VMEM((_RING_SLOTS, MOBA_DIM, chunk_keys), F32),
               pltpu.SemaphoreType.DMA((_RING_SLOTS,)),
               pltpu.VMEM((q_rows, past_len + LANES), F32)]
    blocks = (2 * 2 * _nbytes((tm, D_MODEL), F32) + 3 * _nbytes((D_MODEL, D_FF), BF16)
              + _nbytes((tm, D_FF), BF16)
              + _nbytes((_RING_SLOTS, MOBA_DIM, chunk_keys), F32)
              + _nbytes((q_rows, past_len + LANES), F32) + 8 * _nbytes((SAMPLE_ROWS, MOBA_DIM), F32))
    return pl.pallas_call(
        functools.partial(_ffn_moba_kernel, page_base=layer * n_pool, n_pages=n_pages,
                          page_size=page_size, n_q=n_q),
        grid_spec=pltpu.PrefetchScalarGridSpec(
            num_scalar_prefetch=1,
            grid=(n_req,),
            in_specs=[tile, _resident((1, D_MODEL)), _layer_resident(w1, layer),
                      _layer_resident(w3, layer), _layer_resident(w2, layer),
                      rows, rows, rows, hbm, hbm],
            out_specs=[tile, rows],
            scratch_shapes=scratch),
        out_shape=[jax.ShapeDtypeStruct((m, D_MODEL), F32),
                   jax.ShapeDtypeStruct((n_req * SAMPLE_ROWS, MOBA_DIM), F32)],
        compiler_params=pltpu.CompilerParams(
            dimension_semantics=("arbitrary",), vmem_limit_bytes=_vmem_limit(blocks)),
        name="ffn_moba",
    )(page_table.reshape(-1), x, g, w1, w3, w2, q, k_new, v_new, ck, cv)


def _sample_sgu_tables(sgu_w, sgu_b, n_b, t_new):
    t = jnp.arange(SAMPLE_ROWS)
    d = jnp.arange(t_new)
    src = t[None, :] - d[:, None]
    ok = (src >= 0) & (t[None, :] < t_new)
    w_td = sgu_w[:, t[None, :].clip(0, t_new - 1), src.clip(0, t_new - 1)]
    w_td = jnp.where(ok[None], w_td, 0.0)
    wd = jnp.repeat(jnp.transpose(w_td, (1, 2, 0)), SGU_GROUP_DIM, axis=2)
    bl = jnp.where((t < t_new)[:, None],
                   jnp.repeat(sgu_b[:, t.clip(0, t_new - 1)].T, SGU_GROUP_DIM, axis=1), 0.0)
    return jnp.tile(wd, (1, n_b, 1)), jnp.tile(bl, (n_b, 1))


def kernel(x_prompt, x_sample, mem_prompt, cache_k, cache_v, cache_mem_k, cache_mem_v, page_table,
           ffn1_norm, ffn1_w1, ffn1_w3, ffn1_w2, mix_norm, w_in, sgu_ln_g, sgu_ln_b, sgu_w, sgu_b,
           mem_norm, w_mem_kv, w_a_out, w_b_out, w_c_out, w_o, ffn2_norm, ffn2_w1, ffn2_w3, ffn2_w2,
           final_norm):
    depth = w_in.shape[0]
    bsz, t_len, _ = x_prompt.shape
    dec_b, t_new, _ = x_sample.shape
    n_pages = page_table.shape[1]
    page_size = cache_k.shape[2]
    past_len = n_pages * page_size
    mem_len = mem_prompt.shape[1]
    assert t_len % 512 == 0 and t_new <= SAMPLE_ROWS and past_len % MOBA_BLOCK == 0
    assert n_pages % _PAGES_PER_CHUNK == 0 and (_PAGES_PER_CHUNK * page_size) % MOBA_BLOCK == 0

    row = lambda a: a.reshape(1, -1)
    fin = row(final_norm)
    f1 = tuple(w.astype(BF16) for w in (ffn1_w1, ffn1_w3, ffn1_w2))
    f2 = tuple(w.astype(BF16) for w in (ffn2_w1, ffn2_w3, ffn2_w2))
    win, wkv = w_in.astype(BF16), w_mem_kv.astype(BF16)
    outs_w = tuple(w.astype(BF16) for w in (w_a_out, w_b_out, w_c_out, w_o))

    tm_p = 512
    cos_p, slo_p, shi_p = _rope_tables(jnp.arange(t_len, dtype=jnp.int32))
    hp = x_prompt.reshape(bsz * t_len, D_MODEL)
    mem = mem_prompt.reshape(bsz * mem_len, D_MODEL)
    kp, vp, mkp, mvp = [], [], [], []

    m_s = dec_b * SAMPLE_ROWS
    pos_s = past_len + jnp.arange(SAMPLE_ROWS, dtype=jnp.int32)
    cos_s, slo_s, shi_s = (jnp.tile(t, (dec_b, 1)) for t in _rope_tables(pos_s))
    hs = jnp.pad(x_sample, ((0, 0), (0, SAMPLE_ROWS - t_new), (0, 0))).reshape(m_s, D_MODEL)
    n_b = 8
    mem_k = cache_mem_k.reshape(depth, dec_b, mem_len * MEM_HEADS, MEM_HEAD_DIM)
    mem_v = cache_mem_v.reshape(depth, dec_b, mem_len * MEM_HEADS, MEM_HEAD_DIM)
    ks_, vs_, sv_ = [], [], []
    unpad = lambda a: a.reshape(dec_b, SAMPLE_ROWS, -1)[:, :t_new]

    for l in range(depth):
        wd, bl = _sample_sgu_tables(sgu_w[l], sgu_b[l], n_b, t_new)
        hs = _ffn(hs, row(ffn1_norm[l]), *f1, fin, layer=l, tm=m_s, final_norm=False)
        u_s, vn_s, q_s, k_s, v_s, cq_s, gate_s = _inproj(
            hs, row(mix_norm[l]), win, row(sgu_ln_g[l]), row(sgu_ln_b[l]), cos_s, slo_s, shi_s,
            layer=l, tm=m_s, t_major_kv=False, act_dtype=F32)
        ya_s, yc_s = _premix_sample(u_s, vn_s, cq_s, mem_k, mem_v, wd, bl, layer=l, n_b=n_b)

        mk, mv = _memkv(mem, row(mem_norm[l]), wkv, layer=l, tm=512)
        hp, yb_s = _ffn_moba(hp, row(ffn1_norm[l]), *f1, page_table, q_s, k_s, v_s, cache_k, cache_v,
                             layer=l, tm=tm_p, n_q=t_new)
        u, vn, q, k, v, cq, gate, k_t, v_t = _inproj(
            hp, row(mix_norm[l]), win, row(sgu_ln_g[l]), row(sgu_ln_b[l]), cos_p, slo_p, shi_p,
            layer=l, tm=tm_p, t_major_kv=True, act_dtype=BF16)
        yb = _moba_prompt(q.reshape(bsz, t_len, MOBA_DIM), k.reshape(bsz, t_len, MOBA_DIM),
                          v.reshape(bsz, t_len, MOBA_DIM), out_dtype=BF16).reshape(bsz * t_len, MOBA_DIM)
        hp = _mix_prompt(u, vn, cq, gate, yb, hp, mk.reshape(bsz, mem_len, MEM_DIM),
                         mv.reshape(bsz, mem_len, MEM_DIM), sgu_w, sgu_b[l].T, *outs_w,
                         layer=l, tm=tm_p, t_len=t_len)
        hp = _ffn(hp, row(ffn2_norm[l]), *f2, fin, layer=l, tm=tm_p, final_norm=(l == depth - 1))
        kp.append(k_t)
        vp.append(v_t)
        mkp.append(mk.reshape(bsz, mem_len, MEM_HEADS, MEM_HEAD_DIM))
        mvp.append(mv.reshape(bsz, mem_len, MEM_HEADS, MEM_HEAD_DIM))

        hs = _merge_sample(hs, ya_s, yb_s, yc_s, gate_s, *outs_w, layer=l)
        hs = _ffn(hs, row(ffn2_norm[l]), *f2, fin, layer=l, tm=m_s, final_norm=(l == depth - 1))
        ks_.append(unpad(k_s).reshape(dec_b, t_new, MOBA_HEADS, MOBA_HEAD_DIM))
        vs_.append(unpad(v_s).reshape(dec_b, t_new, MOBA_HEADS, MOBA_HEAD_DIM))
        sv_.append(unpad(vn_s))
    y_prompt = hp.reshape(bsz, t_len, D_MODEL)
    y_sample = unpad(hs)

    def token_minor_to_rows(ts):
        stacked = jnp.stack(ts).reshape(depth, bsz, MOBA_HEADS, MOBA_HEAD_DIM, t_len)
        return jnp.transpose(stacked, (0, 1, 4, 2, 3))

    return (y_prompt, y_sample, token_minor_to_rows(kp), token_minor_to_rows(vp),
            jnp.stack(mkp), jnp.stack(mvp), jnp.stack(ks_), jnp.stack(vs_), jnp.stack(sv_))
```

```python
import functools

import jax
import jax.numpy as jnp
import numpy as np
from jax import lax
from jax.experimental import pallas as pl
from jax.experimental.pallas import tpu as pltpu

F32 = jnp.float32
BF16 = jnp.bfloat16

D_MODEL = 1024
SGU_GROUPS = 4
SGU_GROUP_DIM = 128
SGU_DIM = 512
CHUNK = 128
MOBA_HEADS = 8
MOBA_HEAD_DIM = 64
MOBA_DIM = 512
MOBA_BLOCK = 256
MOBA_TOPK = 3
MEM_HEADS = 4
MEM_HEAD_DIM = 128
MEM_DIM = 512
D_FF = 2816
ROPE_THETA = 10000.0
NORM_EPS = 1e-6
MASK_VALUE = -1e30
LOG2_E = float(np.log2(np.e))
IN_COLS = 6144
GATE_COL0 = 3072

V7X_VMEM_BYTES = 64 * 1024 * 1024
LANES = 128
SUBLANES = 8
SAMPLE_ROWS = 8
TEMP_ALLOWANCE = 12 * 1024 * 1024

NT_DIMS = (((1,), (1,)), ((), ()))


def _vmem_limit(block_bytes):
    return int(min(block_bytes + TEMP_ALLOWANCE, V7X_VMEM_BYTES - 4 * 1024 * 1024))


def _nbytes(shape, dtype):
    return int(np.prod(shape)) * jnp.dtype(dtype).itemsize


def _resident(shape):
    nd = len(shape)
    return pl.BlockSpec(shape, lambda *_: (0,) * nd, pipeline_mode=pl.Buffered(1))


def _layer_resident(arr, layer):
    nd = arr.ndim
    return pl.BlockSpec((None,) + arr.shape[1:], lambda *_: (layer,) + (0,) * (nd - 1),
                        pipeline_mode=pl.Buffered(1))


def _dot(a, b):
    return jnp.dot(a, b, preferred_element_type=F32)


def _dot_nt(a, b, precision=None):
    return lax.dot_general(a, b, NT_DIMS, precision=precision, preferred_element_type=F32)


def _rms(x, g):
    return x * lax.rsqrt(jnp.mean(x * x, axis=-1, keepdims=True) + NORM_EPS) * g


def _sigmoid(x):
    return 1.0 / (1.0 + jnp.exp(-x))


def _gelu(x):
    return 0.5 * x * (1.0 + lax.erf(x * np.float32(np.sqrt(0.5))))


def _softmax_rows(s):
    p = jnp.exp(s - jnp.max(s, axis=-1, keepdims=True))
    return p, 1.0 / jnp.sum(p, axis=-1, keepdims=True)


_FF_CHUNK = 512
_FF_CHUNKS = tuple((s, min(_FF_CHUNK, D_FF - s)) for s in range(0, D_FF, _FF_CHUNK))


def _ffn_kernel(x_ref, g_ref, w1_ref, w3_ref, w2_ref, fg_ref, o_ref, a_ref, *, final_norm):
    x = x_ref[...]
    xb = _rms(x, g_ref[...]).astype(BF16)
    for s, w in _FF_CHUNKS:
        h1 = _dot(xb, w1_ref[:, s:s + w])
        h3 = _dot(xb, w3_ref[:, s:s + w])
        a_ref[:, s:s + w] = (h1 * _sigmoid(h1) * h3).astype(BF16)
    y = x + 0.5 * _dot(a_ref[...], w2_ref[...])
    if final_norm:
        y = _rms(y, fg_ref[...])
    o_ref[...] = y


def _ffn(x, g, w1, w3, w2, fg, *, layer, tm, final_norm):
    m = x.shape[0]
    blocks = (2 * 2 * _nbytes((tm, D_MODEL), F32) + 3 * _nbytes((D_MODEL, D_FF), BF16)
              + _nbytes((tm, D_FF), BF16))
    return pl.pallas_call(
        functools.partial(_ffn_kernel, final_norm=final_norm),
        grid=(m // tm,),
        in_specs=[
            pl.BlockSpec((tm, D_MODEL), lambda i: (i, 0)),
            _resident((1, D_MODEL)),
            _layer_resident(w1, layer),
            _layer_resident(w3, layer),
            _layer_resident(w2, layer),
            _resident((1, D_MODEL)),
        ],
        out_specs=pl.BlockSpec((tm, D_MODEL), lambda i: (i, 0)),
        out_shape=jax.ShapeDtypeStruct((m, D_MODEL), F32),
        scratch_shapes=[pltpu.VMEM((tm, D_FF), BF16)],
        compiler_params=pltpu.CompilerParams(
            dimension_semantics=("parallel",), vmem_limit_bytes=_vmem_limit(blocks)),
        name="ffn",
    )(x, g, w1, w3, w2, fg)


def _rope(h, cos, sin_lo, sin_hi):
    outs = []
    for c in range(h.shape[1] // LANES):
        s = h[:, c * LANES:(c + 1) * LANES]
        outs.append(s * cos + pltpu.roll(s, LANES - 32, 1) * sin_lo + pltpu.roll(s, 32, 1) * sin_hi)
    return jnp.concatenate(outs, axis=1)


def _inproj_kernel(x_ref, g_ref, w_ref, lng_ref, lnb_ref, cos_ref, slo_ref, shi_ref, *refs, n_prev):
    prev_refs, (u_ref, vn_ref, q_ref, k_ref, v_ref, cq_ref, gate_ref), t_refs = (
        refs[:n_prev], refs[n_prev:n_prev + 7], refs[n_prev + 7:])
    xb = _rms(x_ref[...], g_ref[...]).astype(BF16)

    def proj(c0, w=512):
        return _dot(xb, w_ref[:, c0:c0 + w])

    u_ref[...] = _gelu(proj(0)).astype(u_ref.dtype)
    a = _gelu(proj(512))
    ac = a - jnp.mean(a, axis=-1, keepdims=True)
    vn_ref[...] = (ac * lax.rsqrt(jnp.mean(ac * ac, axis=-1, keepdims=True) + NORM_EPS)
                   * lng_ref[...] + lnb_ref[...]).astype(vn_ref.dtype)
    cos, slo, shi = cos_ref[...], slo_ref[...], shi_ref[...]
    q_ref[...] = _rope(proj(1024), cos, slo, shi)
    k = _rope(proj(1536), cos, slo, shi)
    v = proj(2048)
    k_ref[...] = k
    v_ref[...] = v
    for t_ref, new, prev_ref in zip(t_refs, (k, v), prev_refs or (None, None)):
        n_before = t_ref.shape[0] - 1
        if n_before:
            t_ref[:n_before] = prev_ref[...]
        t_ref[n_before] = new.T
    cq_ref[...] = proj(2560).astype(cq_ref.dtype)
    for c in range(6):
        gate_ref[:, c * 512:(c + 1) * 512] = _sigmoid(proj(GATE_COL0 + c * 512)).astype(gate_ref.dtype)


def _inproj(x, g, w, lng, lnb, cos, slo, shi, *, layer, tm, t_major_kv, prev_kv_t, act_dtype):
    m = x.shape[0]
    t_len = cos.shape[0]
    n_tab = t_len // tm
    out_cols = (512,) * 6 + (3 * D_MODEL,)
    blocks = (2 * _nbytes((tm, D_MODEL), F32) + _nbytes((D_MODEL, IN_COLS), BF16)
              + 2 * _nbytes((tm, IN_COLS), F32) + 6 * _nbytes((tm, LANES), F32))
    tab = pl.BlockSpec((tm, LANES), lambda i: (i % n_tab, 0))
    out_specs = [pl.BlockSpec((tm, c), lambda i: (i, 0)) for c in out_cols]
    out_dtypes = (act_dtype, act_dtype, F32, F32, F32, act_dtype, act_dtype)
    out_shape = [jax.ShapeDtypeStruct((m, c), dt) for c, dt in zip(out_cols, out_dtypes)]
    prev_specs = []
    if t_major_kv:
        def stack_spec(n):
            return pl.BlockSpec((n, None, MOBA_DIM, tm), lambda i: (0, i // n_tab, 0, i % n_tab))

        assert len(prev_kv_t) == (2 if layer else 0)
        blocks += 4 * (2 * layer + 1) * _nbytes((MOBA_DIM, tm), F32)
        prev_specs = [stack_spec(layer)] * len(prev_kv_t)
        out_specs += [stack_spec(layer + 1)] * 2
        out_shape += [jax.ShapeDtypeStruct((layer + 1, m // t_len, MOBA_DIM, t_len), F32)] * 2
    return pl.pallas_call(
        functools.partial(_inproj_kernel, n_prev=len(prev_specs)),
        grid=(m // tm,),
        in_specs=[
            pl.BlockSpec((tm, D_MODEL), lambda i: (i, 0)),
            _resident((1, D_MODEL)),
            _layer_resident(w, layer),
            _resident((1, SGU_DIM)),
            _resident((1, SGU_DIM)),
            tab, tab, tab,
        ] + prev_specs,
        out_specs=out_specs,
        out_shape=out_shape,
        compiler_params=pltpu.CompilerParams(
            dimension_semantics=("parallel",), vmem_limit_bytes=_vmem_limit(blocks)),
        name="inproj",
    )(x, g, w, lng, lnb, cos, slo, shi, *prev_kv_t)


def _rope_tables(pos):
    half = MOBA_HEAD_DIM // 2
    inv = ROPE_THETA ** (-jnp.arange(half, dtype=F32) * 2.0 / MOBA_HEAD_DIM)
    ang = pos.astype(F32)[:, None] * inv[None, :]
    cos, sin = jnp.tile(jnp.cos(ang), (1, 4)), jnp.tile(jnp.sin(ang), (1, 4))
    first_half = (jnp.arange(LANES) % MOBA_HEAD_DIM) < half
    return cos, jnp.where(first_half, -sin, 0.0), jnp.where(first_half, 0.0, sin)


def _memkv_kernel(m_ref, g_ref, w_ref, k_ref, v_ref):
    xb = _rms(m_ref[...], g_ref[...]).astype(BF16)
    k_ref[...] = _dot(xb, w_ref[:, :MEM_DIM])
    v_ref[...] = _dot(xb, w_ref[:, MEM_DIM:])


def _memkv(mem, g, w, *, layer, tm):
    m = mem.shape[0]
    blocks = (2 * _nbytes((tm, D_MODEL), F32) + _nbytes((D_MODEL, 2 * MEM_DIM), BF16)
              + 4 * _nbytes((tm, MEM_DIM), F32))
    return pl.pallas_call(
        _memkv_kernel,
        grid=(m // tm,),
        in_specs=[pl.BlockSpec((tm, D_MODEL), lambda i: (i, 0)), _resident((1, D_MODEL)),
                  _layer_resident(w, layer)],
        out_specs=[pl.BlockSpec((tm, MEM_DIM), lambda i: (i, 0))] * 2,
        out_shape=[jax.ShapeDtypeStruct((m, MEM_DIM), F32)] * 2,
        compiler_params=pltpu.CompilerParams(
            dimension_semantics=("parallel",), vmem_limit_bytes=_vmem_limit(blocks)),
        name="memkv",
    )(mem, g, w)


def _topk_select(g, n_past, n_rows):
    row = lax.broadcasted_iota(jnp.int32, g.shape, 0)
    rank = jnp.zeros(g.shape, F32)
    for m in range(n_past):
        gm = g[m:m + 1, :]
        beats = (gm > g) | ((gm == g) & (row > m))
        rank = rank + beats.astype(F32)
    return (row < n_past) & (rank < MOBA_TOPK)


_MOBA_SLABS = 2


def _moba_prompt_kernel(q_ref, k_ref, v_ref, o_ref):
    t_len = q_ref.shape[1]
    n_blk = t_len // MOBA_BLOCK
    n_slabs = q_ref.shape[2] // LANES
    lane = lax.broadcasted_iota(jnp.int32, (1, LANES), 1)
    head_masks = (lane < MOBA_HEAD_DIM, lane >= MOBA_HEAD_DIM)
    r_i = lax.broadcasted_iota(jnp.int32, (2 * MOBA_BLOCK, MOBA_BLOCK), 0) & (MOBA_BLOCK - 1)
    c_i = lax.broadcasted_iota(jnp.int32, (2 * MOBA_BLOCK, MOBA_BLOCK), 1)
    causal2 = c_i <= r_i
    scale = MOBA_HEAD_DIM ** -0.5 * LOG2_E
    ones = jnp.ones((t_len, LANES), BF16)

    kb, v_aug, k_mean = [], [], []
    for sl in range(n_slabs):
        k = k_ref[0, :, sl * LANES:(sl + 1) * LANES]
        kb.append(k.astype(BF16))
        v_aug.append(jnp.concatenate(
            [v_ref[0, :, sl * LANES:(sl + 1) * LANES].astype(BF16), ones], axis=1))
        k_mean.append(jnp.sum(k.reshape(n_blk, MOBA_BLOCK, LANES), axis=1) * (1.0 / MOBA_BLOCK))

    def scores(sl, qi):
        q_blk = q_ref[0, qi * MOBA_BLOCK:(qi + 1) * MOBA_BLOCK, sl * LANES:(sl + 1) * LANES]
        q2 = jnp.concatenate([jnp.where(hm, q_blk, 0.0) for hm in head_masks], axis=0)
        return q2, _dot_nt((q2 * scale).astype(BF16), kb[sl][:(qi + 1) * MOBA_BLOCK])

    def probabilities(sl, qi, q2, s):
        parts = []
        if qi > MOBA_TOPK:
            gate = _dot_nt(k_mean[sl], q2, precision=lax.Precision.HIGHEST)
            sel = _topk_select(gate, qi, n_blk).astype(F32)
            sel_t = jnp.concatenate(
                [sel, jnp.zeros((LANES - n_blk, 2 * MOBA_BLOCK), F32)], axis=0).T
            for n in range(qi):
                keep = sel_t[:, n:n + 1] > 0.5
                parts.append(jnp.where(keep, s[:, n * MOBA_BLOCK:(n + 1) * MOBA_BLOCK], MASK_VALUE))
        else:
            for n in range(qi):
                parts.append(s[:, n * MOBA_BLOCK:(n + 1) * MOBA_BLOCK])
        parts.append(jnp.where(causal2, s[:, qi * MOBA_BLOCK:], MASK_VALUE))
        sm = jnp.concatenate(parts, axis=1) if len(parts) > 1 else parts[0]
        return jnp.exp2(sm - jnp.max(sm, axis=-1, keepdims=True)).astype(BF16)

    def weighted_values(sl, qi, p):
        o2 = _dot(p, v_aug[sl][:(qi + 1) * MOBA_BLOCK])
        o2 = o2[:, :LANES] * (1.0 / o2[:, LANES:])
        o_ref[0, qi * MOBA_BLOCK:(qi + 1) * MOBA_BLOCK, sl * LANES:(sl + 1) * LANES] = jnp.where(
            head_masks[0], o2[:MOBA_BLOCK], o2[MOBA_BLOCK:]).astype(o_ref.dtype)

    slabs = range(n_slabs)
    ahead = [scores(sl, 0) for sl in slabs]
    pending = None
    for qi in range(n_blk):
        cur = ahead
        if qi + 1 < n_blk:
            ahead = [scores(sl, qi + 1) for sl in slabs]
        probs = [probabilities(sl, qi, *cur[sl]) for sl in slabs]
        if pending is not None:
            for sl in slabs:
                weighted_values(sl, qi - 1, pending[sl])
        pending = probs
    for sl in slabs:
        weighted_values(sl, n_blk - 1, pending[sl])


def _moba_prompt(q, k, v, *, out_dtype):
    bsz, t_len, _ = q.shape
    width = _MOBA_SLABS * LANES
    spec = pl.BlockSpec((1, t_len, width), lambda b, h: (b, 0, h))
    blocks = _MOBA_SLABS * (8 * _nbytes((t_len, LANES), F32) + 3 * _nbytes((t_len, LANES), BF16)
                            + 8 * _nbytes((MOBA_BLOCK, t_len), F32))
    return pl.pallas_call(
        _moba_prompt_kernel,
        grid=(bsz, MOBA_DIM // width),
        in_specs=[spec, spec, spec],
        out_specs=spec,
        out_shape=jax.ShapeDtypeStruct((bsz, t_len, MOBA_DIM), out_dtype),
        compiler_params=pltpu.CompilerParams(
            dimension_semantics=("parallel", "parallel"), vmem_limit_bytes=_vmem_limit(blocks)),
        name="moba_prompt",
    )(q, k, v)


def _merge(x, ya, yb, yc, gate_ref, wa_ref, wb_ref, wc_ref, wo_ref):
    t = (gate_ref[:, 0:D_MODEL] * _dot(ya.astype(BF16), wa_ref[...])
         + gate_ref[:, D_MODEL:2 * D_MODEL] * _dot(yb.astype(BF16), wb_ref[...])
         + gate_ref[:, 2 * D_MODEL:3 * D_MODEL] * _dot(yc.astype(BF16), wc_ref[...]))
    return x + _dot(t.astype(BF16), wo_ref[...])


def _mem_attention(cq, mkb, mvb):
    scale = MEM_HEAD_DIM ** -0.5
    outs = []
    for h in range(MEM_HEADS):
        sl = slice(h * MEM_HEAD_DIM, (h + 1) * MEM_HEAD_DIM)
        s = _dot_nt(cq[:, sl].astype(BF16), mkb[:, sl]) * scale
        p, inv_l = _softmax_rows(s)
        outs.append(_dot(p.astype(BF16), mvb[:, sl]) * inv_l)
    return jnp.concatenate(outs, axis=1)


def _mix_prompt_kernel(u_ref, vn_ref, cq_ref, gate_ref, yb_ref, x_ref, mk_ref, mv_ref,
                       sw_ref, sb_ref, wa_ref, wb_ref, wc_ref, wo_ref, o_ref):
    tm = u_ref.shape[0]
    r_i = lax.broadcasted_iota(jnp.int32, (CHUNK, CHUNK), 0)
    c_i = lax.broadcasted_iota(jnp.int32, (CHUNK, CHUNK), 1)
    wm = [jnp.where(c_i <= r_i, sw_ref[g], 0.0).astype(BF16) for g in range(SGU_GROUPS)]
    rows = []
    for ch in range(tm // CHUNK):
        rs = slice(ch * CHUNK, (ch + 1) * CHUNK)
        cols = []
        for g in range(SGU_GROUPS):
            cs = slice(g * SGU_GROUP_DIM, (g + 1) * SGU_GROUP_DIM)
            y = _dot(wm[g], vn_ref[rs, cs].astype(BF16)) + sb_ref[:, g:g + 1]
            cols.append(u_ref[rs, cs].astype(F32) * y)
        rows.append(jnp.concatenate(cols, axis=1))
    ya = jnp.concatenate(rows, axis=0)
    yc = _mem_attention(cq_ref[...], mk_ref[0].astype(BF16), mv_ref[0].astype(BF16))
    o_ref[...] = _merge(x_ref[...], ya, yb_ref[...], yc, gate_ref, wa_ref, wb_ref, wc_ref, wo_ref)


def _mix_prompt(u, vn, cq, gate, yb, x, mk, mv, sw, sb_t, wa, wb, wc, wo, *, layer, tm, t_len):
    m = x.shape[0]
    per_b = t_len // tm

    def rows(c):
        return pl.BlockSpec((tm, c), lambda i: (i, 0))

    mem_spec = pl.BlockSpec((1,) + mk.shape[1:], lambda i: (i // per_b, 0, 0))
    blocks = (2 * _nbytes((tm, 4 * 512 + 3 * D_MODEL + 2 * D_MODEL), F32)
              + 4 * _nbytes(mk.shape[1:], F32) + _nbytes((3 * 512 + D_MODEL, D_MODEL), BF16))
    return pl.pallas_call(
        _mix_prompt_kernel,
        grid=(m // tm,),
        in_specs=[rows(512), rows(512), rows(512), rows(3 * D_MODEL), rows(512), rows(D_MODEL),
                  mem_spec, mem_spec,
                  _layer_resident(sw, layer), _resident(sb_t.shape),
                  _layer_resident(wa, layer), _layer_resident(wb, layer),
                  _layer_resident(wc, layer), _layer_resident(wo, layer)],
        out_specs=rows(D_MODEL),
        out_shape=jax.ShapeDtypeStruct((m, D_MODEL), F32),
        compiler_params=pltpu.CompilerParams(
            dimension_semantics=("parallel",), vmem_limit_bytes=_vmem_limit(blocks)),
        name="mix_prompt",
    )(u, vn, cq, gate, yb, x, mk, mv, sw, sb_t, wa, wb, wc, wo)


def _premix_sample_kernel(u_ref, vn_ref, cq_ref, mk_ref, mv_ref, wd_ref, bl_ref, ya_ref, yc_ref):
    vn = vn_ref[...]
    y = bl_ref[...] + wd_ref[0] * vn
    for d in range(1, wd_ref.shape[0]):
        y = y + wd_ref[d] * pltpu.roll(vn, d, 0)
    ya_ref[...] = u_ref[...] * y

    q_rows = MEM_HEADS * SAMPLE_ROWS
    n_keys = mk_ref.shape[1]
    row_head = lax.broadcasted_iota(jnp.int32, (q_rows, n_keys), 0) // SAMPLE_ROWS
    key_head = lax.broadcasted_iota(jnp.int32, (q_rows, n_keys), 1) % MEM_HEADS
    same_head = row_head == key_head
    scale = MEM_HEAD_DIM ** -0.5
    for b in range(mk_ref.shape[0]):
        rs = slice(b * SAMPLE_ROWS, (b + 1) * SAMPLE_ROWS)
        q8 = cq_ref[rs, :]
        q_h = jnp.concatenate([q8[:, h * MEM_HEAD_DIM:(h + 1) * MEM_HEAD_DIM]
                               for h in range(MEM_HEADS)], axis=0)
        s = _dot_nt(q_h.astype(BF16), mk_ref[b].astype(BF16)) * scale
        p, inv_l = _softmax_rows(jnp.where(same_head, s, MASK_VALUE))
        o = _dot(p.astype(BF16), mv_ref[b].astype(BF16)) * inv_l
        yc_ref[rs, :] = jnp.concatenate([o[h * SAMPLE_ROWS:(h + 1) * SAMPLE_ROWS, :]
                                         for h in range(MEM_HEADS)], axis=1)


def _premix_sample(u, vn, cq, mk, mv, wd, bl, *, layer, n_b):
    m = u.shape[0]
    tm = n_b * SAMPLE_ROWS
    rows = pl.BlockSpec((tm, 512), lambda i: (i, 0))
    mem_block = (n_b,) + mk.shape[2:]
    mem_spec = pl.BlockSpec((None,) + mem_block, lambda i: (layer, i, 0, 0))
    blocks = (2 * 5 * _nbytes((tm, 512), F32) + 4 * _nbytes(mem_block, F32)
              + _nbytes(wd.shape, F32) + _nbytes(bl.shape, F32))
    return pl.pallas_call(
        _premix_sample_kernel,
        grid=(m // tm,),
        in_specs=[rows, rows, rows, mem_spec, mem_spec, _resident(wd.shape), _resident(bl.shape)],
        out_specs=[rows, rows],
        out_shape=[jax.ShapeDtypeStruct((m, 512), F32)] * 2,
        compiler_params=pltpu.CompilerParams(
            dimension_semantics=("parallel",), vmem_limit_bytes=_vmem_limit(blocks)),
        name="premix_sample",
    )(u, vn, cq, mk, mv, wd, bl)


def _merge_kernel(x_ref, ya_ref, yb_ref, yc_ref, gate_ref, wa_ref, wb_ref, wc_ref, wo_ref, o_ref):
    o_ref[...] = _merge(x_ref[...], ya_ref[...], yb_ref[...], yc_ref[...], gate_ref,
                        wa_ref, wb_ref, wc_ref, wo_ref)


def _merge_sample(x, ya, yb, yc, gate, wa, wb, wc, wo, *, layer):
    m = x.shape[0]
    acts, weights = (x, ya, yb, yc, gate), (wa, wb, wc, wo)
    args = acts + weights
    blocks = (sum(_nbytes(a.shape, a.dtype) for a in acts) + _nbytes(x.shape, F32)
              + sum(_nbytes(w.shape[1:], w.dtype) for w in weights))
    return pl.pallas_call(
        _merge_kernel,
        grid=(1,),
        in_specs=[_resident(a.shape) for a in acts] + [_layer_resident(w, layer) for w in weights],
        out_specs=pl.BlockSpec((m, D_MODEL), lambda i: (0, 0)),
        out_shape=jax.ShapeDtypeStruct((m, D_MODEL), F32),
        compiler_params=pltpu.CompilerParams(
            dimension_semantics=("arbitrary",), vmem_limit_bytes=_vmem_limit(blocks)),
        name="merge_sample",
    )(*args)


_PAGES_PER_CHUNK = 16
_RING_SLOTS = 4
_RING_AHEAD = _RING_SLOTS - 1


def _moba_sample_steps(pt_ref, q_ref, kn_ref, vnew_ref, ck_hbm, cv_hbm, o_ref,
                       buf, sem, s_ref, *, page_base, n_pages, page_size, n_q):
    b = pl.program_id(0)
    n_req = pl.num_programs(0)
    chunk_keys = _PAGES_PER_CHUNK * page_size
    n_chunks = n_pages // _PAGES_PER_CHUNK
    blocks_per_chunk = chunk_keys // MOBA_BLOCK
    n_past_blk = n_chunks * blocks_per_chunk
    past_len = n_pages * page_size
    q_rows = n_q * MOBA_HEADS

    def copies(req, c, slot):
        src = ck_hbm if c < n_chunks else cv_hbm
        first = req * n_pages + (c % n_chunks) * _PAGES_PER_CHUNK
        return [pltpu.make_async_copy(src.at[pt_ref[first + p] + page_base],
                                      buf.at[slot, :, pl.ds(p * page_size, page_size)],
                                      sem.at[slot])
                for p in range(_PAGES_PER_CHUNK)]

    def start(req, c, slot):
        for cp in copies(req, c, slot):
            cp.start()

    def wait(req, c, slot):
        for cp in copies(req, c, slot):
            cp.wait()

    n_total = 2 * n_chunks

    @pl.when(b == 0)
    def _():
        for c0 in range(_RING_AHEAD):
            start(b, c0, c0 % _RING_SLOTS)

    sub_i = lax.broadcasted_iota(jnp.int32, (SAMPLE_ROWS, MOBA_DIM), 0)
    own_head = sub_i == lax.broadcasted_iota(jnp.int32, (SAMPLE_ROWS, MOBA_DIM), 1) // MOBA_HEAD_DIM
    q8 = q_ref[...] * (MOBA_HEAD_DIM ** -0.5)
    q_exp = jnp.concatenate(
        [jnp.where(own_head, jnp.broadcast_to(q8[t:t + 1, :], (MOBA_HEADS, MOBA_DIM)), 0.0)
         for t in range(n_q)], axis=0)
    q_exp_b = q_exp.astype(BF16)

    def pad_rows(x8):
        return jnp.concatenate([x8, jnp.zeros((LANES - SAMPLE_ROWS, x8.shape[1]), F32)], axis=0)

    blk_lane = lax.broadcasted_iota(jnp.int32, (MOBA_DIM, LANES), 1)
    ksum = jnp.zeros((MOBA_DIM, LANES), F32)
    acc = jnp.zeros((q_rows, MOBA_DIM), F32)
    inv_l = None
    for c in range(n_total):
        slot = c % _RING_SLOTS
        wait(b, c, slot)
        nxt = c + _RING_AHEAD
        if nxt < n_total:
            start(b, nxt, nxt % _RING_SLOTS)
        else:
            @pl.when(b + 1 < n_req)
            def _(nxt=nxt):
                start(b + 1, nxt - n_total, nxt % _RING_SLOTS)

        if c < n_chunks:
            kc = buf[slot]
            for j in range(blocks_per_chunk):
                col = jnp.sum(kc[:, j * MOBA_BLOCK:(j + 1) * MOBA_BLOCK], axis=1, keepdims=True)
                ksum = jnp.where(blk_lane == c * blocks_per_chunk + j, col, ksum)
            s_ref[:, c * chunk_keys:(c + 1) * chunk_keys] = _dot(q_exp_b, kc.astype(BF16))
            if c == n_chunks - 1:
                gate = jnp.dot(q_exp, ksum * (1.0 / MOBA_BLOCK), precision=lax.Precision.HIGHEST,
                               preferred_element_type=F32)
                lane = lax.broadcasted_iota(jnp.int32, gate.shape, 1)
                rank = jnp.zeros(gate.shape, F32)
                for m in range(n_past_blk):
                    gm = gate[:, m:m + 1]
                    beats = (gm > gate) | ((gm == gate) & (lane > m))
                    rank = rank + beats.astype(F32)
                sel = ((lane < n_past_blk) & (rank < MOBA_TOPK)).astype(F32)
                s_own = _dot_nt(q_exp_b, pad_rows(kn_ref[...]).astype(BF16))
                t_i = lax.broadcasted_iota(jnp.int32, s_own.shape, 0) // MOBA_HEADS
                s_own = jnp.where(lane <= t_i, s_own, MASK_VALUE)
                mx_acc = jnp.full((q_rows, MOBA_BLOCK), MASK_VALUE, F32)
                for n in range(n_past_blk):
                    cs = slice(n * MOBA_BLOCK, (n + 1) * MOBA_BLOCK)
                    blk = jnp.where(sel[:, n:n + 1] > 0.5, s_ref[:, cs], MASK_VALUE)
                    s_ref[:, cs] = blk
                    mx_acc = jnp.maximum(mx_acc, blk)
                mx = jnp.maximum(jnp.max(mx_acc, axis=1, keepdims=True),
                                 jnp.max(s_own, axis=1, keepdims=True))
                p_own = jnp.exp(s_own - mx)
                l_acc = jnp.zeros((q_rows, MOBA_BLOCK), F32)
                for n in range(n_past_blk):
                    cs = slice(n * MOBA_BLOCK, (n + 1) * MOBA_BLOCK)
                    blk = jnp.exp(s_ref[:, cs] - mx)
                    s_ref[:, cs] = blk
                    l_acc = l_acc + blk
                s_ref[:, past_len:past_len + LANES] = p_own
                inv_l = 1.0 / (jnp.sum(l_acc, axis=1, keepdims=True)
                               + jnp.sum(p_own, axis=1, keepdims=True))
        else:
            cc = c - n_chunks
            p_c = s_ref[:, cc * chunk_keys:(cc + 1) * chunk_keys].astype(BF16)
            acc = acc + _dot_nt(p_c, buf[slot].astype(BF16))
        yield c

    p_own = s_ref[:, past_len:past_len + LANES].astype(BF16)
    acc = (acc + _dot(p_own, pad_rows(vnew_ref[...]).astype(BF16))) * inv_l
    y = jnp.zeros((SAMPLE_ROWS, MOBA_DIM), F32)
    for t in range(n_q):
        grp = jnp.where(own_head, acc[t * MOBA_HEADS:(t + 1) * MOBA_HEADS, :], 0.0)
        y = jnp.where(sub_i == t, jnp.sum(grp, axis=0, keepdims=True), y)
    o_ref[...] = y


def _ffn_steps(x_ref, g_ref, w1_ref, w3_ref, w2_ref, o_ref, a_ref):
    x = x_ref[...]
    xb = _rms(x, g_ref[...]).astype(BF16)

    def hidden(s, w):
        h1 = _dot(xb, w1_ref[:, s:s + w])
        h3 = _dot(xb, w3_ref[:, s:s + w])
        a_ref[:, s:s + w] = (h1 * _sigmoid(h1) * h3).astype(BF16)

    def out_cols(c0, c1):
        o_ref[:, c0:c1] = x[:, c0:c1] + 0.5 * _dot(a_ref[...], w2_ref[:, c0:c1])

    half = D_MODEL // 2
    return ([functools.partial(hidden, s, w) for s, w in _FF_CHUNKS]
            + [functools.partial(out_cols, 0, half), functools.partial(out_cols, half, D_MODEL)])


def _ffn_moba_kernel(pt_ref, x_ref, g_ref, w1_ref, w3_ref, w2_ref, q_ref, kn_ref, vnew_ref,
                     ck_hbm, cv_hbm, o_ref, yb_ref, a_ref, buf, sem, s_ref, **moba_params):
    ffn = _ffn_steps(x_ref, g_ref, w1_ref, w3_ref, w2_ref, o_ref, a_ref)
    moba = _moba_sample_steps(pt_ref, q_ref, kn_ref, vnew_ref, ck_hbm, cv_hbm, yb_ref,
                              buf, sem, s_ref, **moba_params)
    n_chunk_steps = 2 * moba_params["n_pages"] // _PAGES_PER_CHUNK
    every = max(1, n_chunk_steps // len(ffn))
    for c in moba:
        if (c + 1) % every == 0 and ffn:
            ffn.pop(0)()
    for piece in ffn:
        piece()


def _ffn_moba(x, g, w1, w3, w2, page_table, q, k_new, v_new, cache_k, cache_v, *, layer, tm, n_q):
    m = x.shape[0]
    n_req, n_pages = page_table.shape
    assert m // tm == n_req
    depth, n_pool, page_size = cache_k.shape[:3]
    ck = jnp.transpose(cache_k, (0, 1, 3, 4, 2)).reshape(depth * n_pool, MOBA_DIM, page_size)
    cv = jnp.transpose(cache_v, (0, 1, 3, 4, 2)).reshape(depth * n_pool, MOBA_DIM, page_size)
    chunk_keys = _PAGES_PER_CHUNK * page_size
    past_len = n_pages * page_size
    q_rows = n_q * MOBA_HEADS
    assert (2 * n_pages // _PAGES_PER_CHUNK) % _RING_SLOTS == 0
    tile = pl.BlockSpec((tm, D_MODEL), lambda i, pt: (i, 0))
    rows = pl.BlockSpec((SAMPLE_ROWS, MOBA_DIM), lambda i, pt: (i, 0))
    hbm = pl.BlockSpec(memory_space=pl.ANY)
    scratch = [pltpu.VMEM((tm, D_FF), BF16),
               pltpu.VMEM((_RING_SLOTS, MOBA_DIM, chunk_keys), F32),
               pltpu.SemaphoreType.DMA((_RING_SLOTS,)),
               pltpu.VMEM((q_rows, past_len + LANES), F32)]
    blocks = (2 * 2 * _nbytes((tm, D_MODEL), F32) + 3 * _nbytes((D_MODEL, D_FF), BF16)
              + _nbytes((tm, D_FF), BF16)
              + _nbytes((_RING_SLOTS, MOBA_DIM, chunk_keys), F32)
              + _nbytes((q_rows, past_len + LANES), F32) + 8 * _nbytes((SAMPLE_ROWS, MOBA_DIM), F32))
    return pl.pallas_call(
        functools.partial(_ffn_moba_kernel, page_base=layer * n_pool, n_pages=n_pages,
                          page_size=page_size, n_q=n_q),
        grid_spec=pltpu.PrefetchScalarGridSpec(
            num_scalar_prefetch=1,
            grid=(n_req,),
            in_specs=[tile, _resident((1, D_MODEL)), _layer_resident(w1, layer),
                      _layer_resident(w3, layer), _layer_resident(w2, layer),
                      rows, rows, rows, hbm, hbm],
            out_specs=[tile, rows],
            scratch_shapes=scratch),
        out_shape=[jax.ShapeDtypeStruct((m, D_MODEL), F32),
                   jax.ShapeDtypeStruct((n_req * SAMPLE_ROWS, MOBA_DIM), F32)],
        compiler_params=pltpu.CompilerParams(
            dimension_semantics=("arbitrary",), vmem_limit_bytes=_vmem_limit(blocks)),
        name="ffn_moba",
    )(page_table.reshape(-1), x, g, w1, w3, w2, q, k_new, v_new, ck, cv)


def _sample_sgu_tables(sgu_w, sgu_b, n_b, t_new):
    t = jnp.arange(SAMPLE_ROWS)
    d = jnp.arange(t_new)
    src = t[None, :] - d[:, None]
    ok = (src >= 0) & (t[None, :] < t_new)
    w_td = sgu_w[:, t[None, :].clip(0, t_new - 1), src.clip(0, t_new - 1)]
    w_td = jnp.where(ok[None], w_td, 0.0)
    wd = jnp.repeat(jnp.transpose(w_td, (1, 2, 0)), SGU_GROUP_DIM, axis=2)
    bl = jnp.where((t < t_new)[:, None],
                   jnp.repeat(sgu_b[:, t.clip(0, t_new - 1)].T, SGU_GROUP_DIM, axis=1), 0.0)
    return jnp.tile(wd, (1, n_b, 1)), jnp.tile(bl, (n_b, 1))


def kernel(x_prompt, x_sample, mem_prompt, cache_k, cache_v, cache_mem_k, cache_mem_v, page_table,
           ffn1_norm, ffn1_w1, ffn1_w3, ffn1_w2, mix_norm, w_in, sgu_ln_g, sgu_ln_b, sgu_w, sgu_b,
           mem_norm, w_mem_kv, w_a_out, w_b_out, w_c_out, w_o, ffn2_norm, ffn2_w1, ffn2_w3, ffn2_w2,
           final_norm):
    depth = w_in.shape[0]
    bsz, t_len, _ = x_prompt.shape
    dec_b, t_new, _ = x_sample.shape
    n_pages = page_table.shape[1]
    page_size = cache_k.shape[2]
    past_len = n_pages * page_size
    mem_len = mem_prompt.shape[1]
    assert t_len % 512 == 0 and t_new <= SAMPLE_ROWS and past_len % MOBA_BLOCK == 0
    assert n_pages % _PAGES_PER_CHUNK == 0 and (_PAGES_PER_CHUNK * page_size) % MOBA_BLOCK == 0

    row = lambda a: a.reshape(1, -1)
    fin = row(final_norm)
    f1 = tuple(w.astype(BF16) for w in (ffn1_w1, ffn1_w3, ffn1_w2))
    f2 = tuple(w.astype(BF16) for w in (ffn2_w1, ffn2_w3, ffn2_w2))
    win, wkv = w_in.astype(BF16), w_mem_kv.astype(BF16)
    outs_w = tuple(w.astype(BF16) for w in (w_a_out, w_b_out, w_c_out, w_o))

    tm_p = 512
    cos_p, slo_p, shi_p = _rope_tables(jnp.arange(t_len, dtype=jnp.int32))
    hp = x_prompt.reshape(bsz * t_len, D_MODEL)
    mem = mem_prompt.reshape(bsz * mem_len, D_MODEL)
    kv_t, mkp, mvp = (), [], []

    m_s = dec_b * SAMPLE_ROWS
    pos_s = past_len + jnp.arange(SAMPLE_ROWS, dtype=jnp.int32)
    cos_s, slo_s, shi_s = (jnp.tile(t, (dec_b, 1)) for t in _rope_tables(pos_s))
    hs = jnp.pad(x_sample, ((0, 0), (0, SAMPLE_ROWS - t_new), (0, 0))).reshape(m_s, D_MODEL)
    n_b = 8
    mem_k = cache_mem_k.reshape(depth, dec_b, mem_len * MEM_HEADS, MEM_HEAD_DIM)
    mem_v = cache_mem_v.reshape(depth, dec_b, mem_len * MEM_HEADS, MEM_HEAD_DIM)
    ks_, vs_, sv_ = [], [], []
    unpad = lambda a: a.reshape(dec_b, SAMPLE_ROWS, -1)[:, :t_new]

    for l in range(depth):
        wd, bl = _sample_sgu_tables(sgu_w[l], sgu_b[l], n_b, t_new)
        hs = _ffn(hs, row(ffn1_norm[l]), *f1, fin, layer=l, tm=m_s, final_norm=False)
        u_s, vn_s, q_s, k_s, v_s, cq_s, gate_s = _inproj(
            hs, row(mix_norm[l]), win, row(sgu_ln_g[l]), row(sgu_ln_b[l]), cos_s, slo_s, shi_s,
            layer=l, tm=m_s, t_major_kv=False, prev_kv_t=(), act_dtype=F32)
        ya_s, yc_s = _premix_sample(u_s, vn_s, cq_s, mem_k, mem_v, wd, bl, layer=l, n_b=n_b)

        mk, mv = _memkv(mem, row(mem_norm[l]), wkv, layer=l, tm=512)
        hp, yb_s = _ffn_moba(hp, row(ffn1_norm[l]), *f1, page_table, q_s, k_s, v_s, cache_k, cache_v,
                             layer=l, tm=tm_p, n_q=t_new)
        u, vn, q, k, v, cq, gate, *kv_t = _inproj(
            hp, row(mix_norm[l]), win, row(sgu_ln_g[l]), row(sgu_ln_b[l]), cos_p, slo_p, shi_p,
            layer=l, tm=tm_p, t_major_kv=True, prev_kv_t=kv_t, act_dtype=BF16)
        yb = _moba_prompt(q.reshape(bsz, t_len, MOBA_DIM), k.reshape(bsz, t_len, MOBA_DIM),
                          v.reshape(bsz, t_len, MOBA_DIM), out_dtype=BF16).reshape(bsz * t_len, MOBA_DIM)
        hp = _mix_prompt(u, vn, cq, gate, yb, hp, mk.reshape(bsz, mem_len, MEM_DIM),
                         mv.reshape(bsz, mem_len, MEM_DIM), sgu_w, sgu_b[l].T, *outs_w,
                         layer=l, tm=tm_p, t_len=t_len)
        hp = _ffn(hp, row(ffn2_norm[l]), *f2, fin, layer=l, tm=tm_p, final_norm=(l == depth - 1))
        mkp.append(mk.reshape(bsz, mem_len, MEM_HEADS, MEM_HEAD_DIM))
        mvp.append(mv.reshape(bsz, mem_len, MEM_HEADS, MEM_HEAD_DIM))

        hs = _merge_sample(hs, ya_s, yb_s, yc_s, gate_s, *outs_w, layer=l)
        hs = _ffn(hs, row(ffn2_norm[l]), *f2, fin, layer=l, tm=m_s, final_norm=(l == depth - 1))
        ks_.append(unpad(k_s).reshape(dec_b, t_new, MOBA_HEADS, MOBA_HEAD_DIM))
        vs_.append(unpad(v_s).reshape(dec_b, t_new, MOBA_HEADS, MOBA_HEAD_DIM))
        sv_.append(unpad(vn_s))
    y_prompt = hp.reshape(bsz, t_len, D_MODEL)
    y_sample = unpad(hs)

    def token_minor_to_rows(stack):
        return jnp.transpose(stack.reshape(depth, bsz, MOBA_HEADS, MOBA_HEAD_DIM, t_len), (0, 1, 4, 2, 3))

    return (y_prompt, y_sample, token_minor_to_rows(kv_t[0]), token_minor_to_rows(kv_t[1]),
            jnp.stack(mkp), jnp.stack(mvp), jnp.stack(ks_), jnp.stack(vs_), jnp.stack(sv_))
```

```python
import functools

import jax
import jax.numpy as jnp
import numpy as np
from jax import lax
from jax.experimental import pallas as pl
from jax.experimental.pallas import tpu as pltpu

F32 = jnp.float32
BF16 = jnp.bfloat16

D_MODEL = 1024
SGU_GROUPS = 4
SGU_GROUP_DIM = 128
SGU_DIM = 512
CHUNK = 128
MOBA_HEADS = 8
MOBA_HEAD_DIM = 64
MOBA_DIM = 512
MOBA_BLOCK = 256
MOBA_TOPK = 3
MEM_HEADS = 4
MEM_HEAD_DIM = 128
MEM_DIM = 512
D_FF = 2816
ROPE_THETA = 10000.0
NORM_EPS = 1e-6
MASK_VALUE = -1e30
LOG2_E = float(np.log2(np.e))
IN_COLS = 6144
GATE_COL0 = 3072

V7X_VMEM_BYTES = 64 * 1024 * 1024
V7X_VMEM_RESERVE = 4 * 1024 * 1024
LANES = 128
SUBLANES = 8
SAMPLE_ROWS = SUBLANES
TEMP_ALLOWANCE = 12 * 1024 * 1024

NT_DIMS = (((1,), (1,)), ((), ()))


def _vmem_limit(block_bytes):
    return int(min(block_bytes + TEMP_ALLOWANCE, V7X_VMEM_BYTES - V7X_VMEM_RESERVE))


def _nbytes(shape, dtype):
    return int(np.prod(shape)) * jnp.dtype(dtype).itemsize


def _resident(shape):
    nd = len(shape)
    return pl.BlockSpec(shape, lambda *_: (0,) * nd, pipeline_mode=pl.Buffered(1))


def _layer_resident(arr, layer):
    nd = arr.ndim
    return pl.BlockSpec((None,) + arr.shape[1:], lambda *_: (layer,) + (0,) * (nd - 1),
                        pipeline_mode=pl.Buffered(1))


def _dot(a, b):
    return jnp.dot(a, b, preferred_element_type=F32)


def _dot_nt(a, b, precision=None):
    return lax.dot_general(a, b, NT_DIMS, precision=precision, preferred_element_type=F32)


def _rms(x, g):
    return x * lax.rsqrt(jnp.mean(x * x, axis=-1, keepdims=True) + NORM_EPS) * g


def _sigmoid(x):
    return 1.0 / (1.0 + jnp.exp(-x))


def _gelu(x):
    return 0.5 * x * (1.0 + lax.erf(x * np.float32(np.sqrt(0.5))))


def _softmax_rows(s):
    p = jnp.exp(s - jnp.max(s, axis=-1, keepdims=True))
    return p, 1.0 / jnp.sum(p, axis=-1, keepdims=True)


_FF_CHUNK = 512
_FF_CHUNKS = tuple((s, min(_FF_CHUNK, D_FF - s)) for s in range(0, D_FF, _FF_CHUNK))


def _ffn_kernel(x_ref, g_ref, w1_ref, w3_ref, w2_ref, fg_ref, o_ref, a_ref, *, final_norm):
    x = x_ref[...]
    xb = _rms(x, g_ref[...]).astype(BF16)
    for s, w in _FF_CHUNKS:
        h1 = _dot(xb, w1_ref[:, s:s + w])
        h3 = _dot(xb, w3_ref[:, s:s + w])
        a_ref[:, s:s + w] = (h1 * _sigmoid(h1) * h3).astype(BF16)
    y = x + 0.5 * _dot(a_ref[...], w2_ref[...])
    if final_norm:
        y = _rms(y, fg_ref[...])
    o_ref[...] = y


def _ffn(x, g, w1, w3, w2, fg, *, layer, tm, final_norm):
    m = x.shape[0]
    blocks = (2 * 2 * _nbytes((tm, D_MODEL), F32) + 3 * _nbytes((D_MODEL, D_FF), BF16)
              + _nbytes((tm, D_FF), BF16))
    return pl.pallas_call(
        functools.partial(_ffn_kernel, final_norm=final_norm),
        grid=(m // tm,),
        in_specs=[
            pl.BlockSpec((tm, D_MODEL), lambda i: (i, 0)),
            _resident((1, D_MODEL)),
            _layer_resident(w1, layer),
            _layer_resident(w3, layer),
            _layer_resident(w2, layer),
            _resident((1, D_MODEL)),
        ],
        out_specs=pl.BlockSpec((tm, D_MODEL), lambda i: (i, 0)),
        out_shape=jax.ShapeDtypeStruct((m, D_MODEL), F32),
        scratch_shapes=[pltpu.VMEM((tm, D_FF), BF16)],
        compiler_params=pltpu.CompilerParams(
            dimension_semantics=("parallel",), vmem_limit_bytes=_vmem_limit(blocks)),
        name="ffn",
    )(x, g, w1, w3, w2, fg)


def _rope(h, cos, sin_lo, sin_hi):
    outs = []
    for c in range(h.shape[1] // LANES):
        s = h[:, c * LANES:(c + 1) * LANES]
        outs.append(s * cos + pltpu.roll(s, LANES - 32, 1) * sin_lo + pltpu.roll(s, 32, 1) * sin_hi)
    return jnp.concatenate(outs, axis=1)


def _inproj_kernel(x_ref, g_ref, w_ref, lng_ref, lnb_ref, cos_ref, slo_ref, shi_ref, *refs, n_prev):
    prev_refs, (u_ref, vn_ref, q_ref, k_ref, v_ref, cq_ref, gate_ref), t_refs = (
        refs[:n_prev], refs[n_prev:n_prev + 7], refs[n_prev + 7:])
    xb = _rms(x_ref[...], g_ref[...]).astype(BF16)

    def proj(c0, w=512):
        return _dot(xb, w_ref[:, c0:c0 + w])

    u_ref[...] = _gelu(proj(0)).astype(u_ref.dtype)
    a = _gelu(proj(512))
    ac = a - jnp.mean(a, axis=-1, keepdims=True)
    vn_ref[...] = (ac * lax.rsqrt(jnp.mean(ac * ac, axis=-1, keepdims=True) + NORM_EPS)
                   * lng_ref[...] + lnb_ref[...]).astype(vn_ref.dtype)
    cos, slo, shi = cos_ref[...], slo_ref[...], shi_ref[...]
    q_ref[...] = _rope(proj(1024), cos, slo, shi)
    k = _rope(proj(1536), cos, slo, shi)
    v = proj(2048)
    k_ref[...] = k
    v_ref[...] = v
    for t_ref, new, prev_ref in zip(t_refs, (k, v), prev_refs or (None, None)):
        n_before = t_ref.shape[0] - 1
        if n_before:
            t_ref[:n_before] = prev_ref[...]
        t_ref[n_before] = new.T
    cq_ref[...] = proj(2560).astype(cq_ref.dtype)
    for c in range(6):
        gate_ref[:, c * 512:(c + 1) * 512] = _sigmoid(proj(GATE_COL0 + c * 512)).astype(gate_ref.dtype)


def _inproj(x, g, w, lng, lnb, cos, slo, shi, *, layer, tm, t_major_kv, prev_kv_t, act_dtype):
    m = x.shape[0]
    t_len = cos.shape[0]
    n_tab = t_len // tm
    out_cols = (512,) * 6 + (3 * D_MODEL,)
    blocks = (2 * _nbytes((tm, D_MODEL), F32) + _nbytes((D_MODEL, IN_COLS), BF16)
              + 2 * _nbytes((tm, IN_COLS), F32) + 6 * _nbytes((tm, LANES), F32))
    tab = pl.BlockSpec((tm, LANES), lambda i: (i % n_tab, 0))
    out_specs = [pl.BlockSpec((tm, c), lambda i: (i, 0)) for c in out_cols]
    out_dtypes = (act_dtype, act_dtype, F32, F32, F32, act_dtype, act_dtype)
    out_shape = [jax.ShapeDtypeStruct((m, c), dt) for c, dt in zip(out_cols, out_dtypes)]
    prev_specs = []
    if t_major_kv:
        def stack_spec(n):
            return pl.BlockSpec((n, None, MOBA_DIM, tm), lambda i: (0, i // n_tab, 0, i % n_tab))

        assert len(prev_kv_t) == (2 if layer else 0)
        blocks += 4 * (2 * layer + 1) * _nbytes((MOBA_DIM, tm), F32)
        prev_specs = [stack_spec(layer)] * len(prev_kv_t)
        out_specs += [stack_spec(layer + 1)] * 2
        out_shape += [jax.ShapeDtypeStruct((layer + 1, m // t_len, MOBA_DIM, t_len), F32)] * 2
    return pl.pallas_call(
        functools.partial(_inproj_kernel, n_prev=len(prev_specs)),
        grid=(m // tm,),
        in_specs=[
            pl.BlockSpec((tm, D_MODEL), lambda i: (i, 0)),
            _resident((1, D_MODEL)),
            _layer_resident(w, layer),
            _resident((1, SGU_DIM)),
            _resident((1, SGU_DIM)),
            tab, tab, tab,
        ] + prev_specs,
        out_specs=out_specs,
        out_shape=out_shape,
        compiler_params=pltpu.CompilerParams(
            dimension_semantics=("parallel",), vmem_limit_bytes=_vmem_limit(blocks)),
        name="inproj",
    )(x, g, w, lng, lnb, cos, slo, shi, *prev_kv_t)


def _rope_tables(pos):
    half = MOBA_HEAD_DIM // 2
    inv = ROPE_THETA ** (-jnp.arange(half, dtype=F32) * 2.0 / MOBA_HEAD_DIM)
    ang = pos.astype(F32)[:, None] * inv[None, :]
    cos, sin = jnp.tile(jnp.cos(ang), (1, 4)), jnp.tile(jnp.sin(ang), (1, 4))
    first_half = (jnp.arange(LANES) % MOBA_HEAD_DIM) < half
    return cos, jnp.where(first_half, -sin, 0.0), jnp.where(first_half, 0.0, sin)


def _memkv_kernel(m_ref, g_ref, w_ref, k_ref, v_ref):
    xb = _rms(m_ref[...], g_ref[...]).astype(BF16)
    k_ref[...] = _dot(xb, w_ref[:, :MEM_DIM])
    v_ref[...] = _dot(xb, w_ref[:, MEM_DIM:])


def _memkv(mem, g, w, *, layer, tm):
    m = mem.shape[0]
    blocks = (2 * _nbytes((tm, D_MODEL), F32) + _nbytes((D_MODEL, 2 * MEM_DIM), BF16)
              + 4 * _nbytes((tm, MEM_DIM), F32))
    return pl.pallas_call(
        _memkv_kernel,
        grid=(m // tm,),
        in_specs=[pl.BlockSpec((tm, D_MODEL), lambda i: (i, 0)), _resident((1, D_MODEL)),
                  _layer_resident(w, layer)],
        out_specs=[pl.BlockSpec((tm, MEM_DIM), lambda i: (i, 0))] * 2,
        out_shape=[jax.ShapeDtypeStruct((m, MEM_DIM), F32)] * 2,
        compiler_params=pltpu.CompilerParams(
            dimension_semantics=("parallel",), vmem_limit_bytes=_vmem_limit(blocks)),
        name="memkv",
    )(mem, g, w)


def _topk_select(g, n_past):
    row = lax.broadcasted_iota(jnp.int32, g.shape, 0)
    rank = jnp.zeros(g.shape, F32)
    for m in range(n_past):
        gm = g[m:m + 1, :]
        beats = (gm > g) | ((gm == g) & (row > m))
        rank = rank + beats.astype(F32)
    return (row < n_past) & (rank < MOBA_TOPK)


_MOBA_SLABS = 2


def _moba_prompt_kernel(q_ref, k_ref, v_ref, o_ref):
    t_len = q_ref.shape[1]
    n_blk = t_len // MOBA_BLOCK
    n_slabs = q_ref.shape[2] // LANES
    lane = lax.broadcasted_iota(jnp.int32, (1, LANES), 1)
    head_masks = (lane < MOBA_HEAD_DIM, lane >= MOBA_HEAD_DIM)
    r_i = lax.broadcasted_iota(jnp.int32, (2 * MOBA_BLOCK, MOBA_BLOCK), 0) & (MOBA_BLOCK - 1)
    c_i = lax.broadcasted_iota(jnp.int32, (2 * MOBA_BLOCK, MOBA_BLOCK), 1)
    causal2 = c_i <= r_i
    scale = MOBA_HEAD_DIM ** -0.5 * LOG2_E
    ones = jnp.ones((t_len, LANES), BF16)

    kb, v_aug, k_mean = [], [], []
    for sl in range(n_slabs):
        k = k_ref[0, :, sl * LANES:(sl + 1) * LANES]
        kb.append(k.astype(BF16))
        v_aug.append(jnp.concatenate(
            [v_ref[0, :, sl * LANES:(sl + 1) * LANES].astype(BF16), ones], axis=1))
        k_mean.append(jnp.sum(k.reshape(n_blk, MOBA_BLOCK, LANES), axis=1) * (1.0 / MOBA_BLOCK))

    def scores(sl, qi):
        q_blk = q_ref[0, qi * MOBA_BLOCK:(qi + 1) * MOBA_BLOCK, sl * LANES:(sl + 1) * LANES]
        q2 = jnp.concatenate([jnp.where(hm, q_blk, 0.0) for hm in head_masks], axis=0)
        return q2, _dot_nt((q2 * scale).astype(BF16), kb[sl][:(qi + 1) * MOBA_BLOCK])

    def probabilities(sl, qi, q2, s):
        parts = []
        if qi > MOBA_TOPK:
            gate = _dot_nt(k_mean[sl], q2, precision=lax.Precision.HIGHEST)
            sel = _topk_select(gate, qi).astype(F32)
            sel_t = jnp.concatenate(
                [sel, jnp.zeros((LANES - n_blk, 2 * MOBA_BLOCK), F32)], axis=0).T
            for n in range(qi):
                keep = sel_t[:, n:n + 1] > 0.5
                parts.append(jnp.where(keep, s[:, n * MOBA_BLOCK:(n + 1) * MOBA_BLOCK], MASK_VALUE))
        else:
            for n in range(qi):
                parts.append(s[:, n * MOBA_BLOCK:(n + 1) * MOBA_BLOCK])
        parts.append(jnp.where(causal2, s[:, qi * MOBA_BLOCK:], MASK_VALUE))
        sm = jnp.concatenate(parts, axis=1) if len(parts) > 1 else parts[0]
        return jnp.exp2(sm - jnp.max(sm, axis=-1, keepdims=True)).astype(BF16)

    def weighted_values(sl, qi, p):
        o2 = _dot(p, v_aug[sl][:(qi + 1) * MOBA_BLOCK])
        o2 = o2[:, :LANES] * (1.0 / o2[:, LANES:])
        o_ref[0, qi * MOBA_BLOCK:(qi + 1) * MOBA_BLOCK, sl * LANES:(sl + 1) * LANES] = jnp.where(
            head_masks[0], o2[:MOBA_BLOCK], o2[MOBA_BLOCK:]).astype(o_ref.dtype)

    slabs = range(n_slabs)
    ahead = [scores(sl, 0) for sl in slabs]
    pending = None
    for qi in range(n_blk):
        cur = ahead
        if qi + 1 < n_blk:
            ahead = [scores(sl, qi + 1) for sl in slabs]
        probs = [probabilities(sl, qi, *cur[sl]) for sl in slabs]
        if pending is not None:
            for sl in slabs:
                weighted_values(sl, qi - 1, pending[sl])
        pending = probs
    for sl in slabs:
        weighted_values(sl, n_blk - 1, pending[sl])


def _moba_prompt(q, k, v, *, out_dtype):
    bsz, t_len, _ = q.shape
    width = _MOBA_SLABS * LANES
    spec = pl.BlockSpec((1, t_len, width), lambda b, h: (b, 0, h))
    blocks = _MOBA_SLABS * (8 * _nbytes((t_len, LANES), F32) + 3 * _nbytes((t_len, LANES), BF16)
                            + 8 * _nbytes((MOBA_BLOCK, t_len), F32))
    return pl.pallas_call(
        _moba_prompt_kernel,
        grid=(bsz, MOBA_DIM // width),
        in_specs=[spec, spec, spec],
        out_specs=spec,
        out_shape=jax.ShapeDtypeStruct((bsz, t_len, MOBA_DIM), out_dtype),
        compiler_params=pltpu.CompilerParams(
            dimension_semantics=("parallel", "parallel"), vmem_limit_bytes=_vmem_limit(blocks)),
        name="moba_prompt",
    )(q, k, v)


def _merge(x, ya, yb, yc, gate_ref, wa_ref, wb_ref, wc_ref, wo_ref):
    t = (gate_ref[:, 0:D_MODEL] * _dot(ya.astype(BF16), wa_ref[...])
         + gate_ref[:, D_MODEL:2 * D_MODEL] * _dot(yb.astype(BF16), wb_ref[...])
         + gate_ref[:, 2 * D_MODEL:3 * D_MODEL] * _dot(yc.astype(BF16), wc_ref[...]))
    return x + _dot(t.astype(BF16), wo_ref[...])


def _mem_attention(cq, mkb, mvb):
    scale = MEM_HEAD_DIM ** -0.5
    outs = []
    for h in range(MEM_HEADS):
        sl = slice(h * MEM_HEAD_DIM, (h + 1) * MEM_HEAD_DIM)
        s = _dot_nt(cq[:, sl].astype(BF16), mkb[:, sl]) * scale
        p, inv_l = _softmax_rows(s)
        outs.append(_dot(p.astype(BF16), mvb[:, sl]) * inv_l)
    return jnp.concatenate(outs, axis=1)


def _mix_prompt_kernel(u_ref, vn_ref, cq_ref, gate_ref, yb_ref, x_ref, mk_ref, mv_ref,
                       sw_ref, sb_ref, wa_ref, wb_ref, wc_ref, wo_ref, o_ref):
    tm = u_ref.shape[0]
    r_i = lax.broadcasted_iota(jnp.int32, (CHUNK, CHUNK), 0)
    c_i = lax.broadcasted_iota(jnp.int32, (CHUNK, CHUNK), 1)
    wm = [jnp.where(c_i <= r_i, sw_ref[g], 0.0).astype(BF16) for g in range(SGU_GROUPS)]
    rows = []
    for ch in range(tm // CHUNK):
        rs = slice(ch * CHUNK, (ch + 1) * CHUNK)
        cols = []
        for g in range(SGU_GROUPS):
            cs = slice(g * SGU_GROUP_DIM, (g + 1) * SGU_GROUP_DIM)
            y = _dot(wm[g], vn_ref[rs, cs].astype(BF16)) + sb_ref[:, g:g + 1]
            cols.append(u_ref[rs, cs].astype(F32) * y)
        rows.append(jnp.concatenate(cols, axis=1))
    ya = jnp.concatenate(rows, axis=0)
    yc = _mem_attention(cq_ref[...], mk_ref[0].astype(BF16), mv_ref[0].astype(BF16))
    o_ref[...] = _merge(x_ref[...], ya, yb_ref[...], yc, gate_ref, wa_ref, wb_ref, wc_ref, wo_ref)


def _mix_prompt(u, vn, cq, gate, yb, x, mk, mv, sw, sb_t, wa, wb, wc, wo, *, layer, tm, t_len):
    m = x.shape[0]
    per_b = t_len // tm

    def rows(c):
        return pl.BlockSpec((tm, c), lambda i: (i, 0))

    mem_spec = pl.BlockSpec((1,) + mk.shape[1:], lambda i: (i // per_b, 0, 0))
    blocks = (2 * _nbytes((tm, 4 * 512 + 3 * D_MODEL + 2 * D_MODEL), F32)
              + 4 * _nbytes(mk.shape[1:], F32) + _nbytes((3 * 512 + D_MODEL, D_MODEL), BF16))
    return pl.pallas_call(
        _mix_prompt_kernel,
        grid=(m // tm,),
        in_specs=[rows(512), rows(512), rows(512), rows(3 * D_MODEL), rows(512), rows(D_MODEL),
                  mem_spec, mem_spec,
                  _layer_resident(sw, layer), _resident(sb_t.shape),
                  _layer_resident(wa, layer), _layer_resident(wb, layer),
                  _layer_resident(wc, layer), _layer_resident(wo, layer)],
        out_specs=rows(D_MODEL),
        out_shape=jax.ShapeDtypeStruct((m, D_MODEL), F32),
        compiler_params=pltpu.CompilerParams(
            dimension_semantics=("parallel",), vmem_limit_bytes=_vmem_limit(blocks)),
        name="mix_prompt",
    )(u, vn, cq, gate, yb, x, mk, mv, sw, sb_t, wa, wb, wc, wo)


def _premix_sample_kernel(u_ref, vn_ref, cq_ref, mk_ref, mv_ref, wd_ref, bl_ref, ya_ref, yc_ref):
    vn = vn_ref[...]
    y = bl_ref[...] + wd_ref[0] * vn
    for d in range(1, wd_ref.shape[0]):
        y = y + wd_ref[d] * pltpu.roll(vn, d, 0)
    ya_ref[...] = u_ref[...] * y

    q_rows = MEM_HEADS * SAMPLE_ROWS
    n_keys = mk_ref.shape[1]
    row_head = lax.broadcasted_iota(jnp.int32, (q_rows, n_keys), 0) // SAMPLE_ROWS
    key_head = lax.broadcasted_iota(jnp.int32, (q_rows, n_keys), 1) % MEM_HEADS
    same_head = row_head == key_head
    scale = MEM_HEAD_DIM ** -0.5
    for b in range(mk_ref.shape[0]):
        rs = slice(b * SAMPLE_ROWS, (b + 1) * SAMPLE_ROWS)
        q8 = cq_ref[rs, :]
        q_h = jnp.concatenate([q8[:, h * MEM_HEAD_DIM:(h + 1) * MEM_HEAD_DIM]
                               for h in range(MEM_HEADS)], axis=0)
        s = _dot_nt(q_h.astype(BF16), mk_ref[b].astype(BF16)) * scale
        p, inv_l = _softmax_rows(jnp.where(same_head, s, MASK_VALUE))
        o = _dot(p.astype(BF16), mv_ref[b].astype(BF16)) * inv_l
        yc_ref[rs, :] = jnp.concatenate([o[h * SAMPLE_ROWS:(h + 1) * SAMPLE_ROWS, :]
                                         for h in range(MEM_HEADS)], axis=1)


def _premix_sample(u, vn, cq, mk, mv, wd, bl, *, layer, n_b):
    m = u.shape[0]
    tm = n_b * SAMPLE_ROWS
    rows = pl.BlockSpec((tm, 512), lambda i: (i, 0))
    mem_block = (n_b,) + mk.shape[2:]
    mem_spec = pl.BlockSpec((None,) + mem_block, lambda i: (layer, i, 0, 0))
    blocks = (2 * 5 * _nbytes((tm, 512), F32) + 4 * _nbytes(mem_block, F32)
              + _nbytes(wd.shape, F32) + _nbytes(bl.shape, F32))
    return pl.pallas_call(
        _premix_sample_kernel,
        grid=(m // tm,),
        in_specs=[rows, rows, rows, mem_spec, mem_spec, _resident(wd.shape), _resident(bl.shape)],
        out_specs=[rows, rows],
        out_shape=[jax.ShapeDtypeStruct((m, 512), F32)] * 2,
        compiler_params=pltpu.CompilerParams(
            dimension_semantics=("parallel",), vmem_limit_bytes=_vmem_limit(blocks)),
        name="premix_sample",
    )(u, vn, cq, mk, mv, wd, bl)


def _merge_kernel(x_ref, ya_ref, yb_ref, yc_ref, gate_ref, wa_ref, wb_ref, wc_ref, wo_ref, o_ref):
    o_ref[...] = _merge(x_ref[...], ya_ref[...], yb_ref[...], yc_ref[...], gate_ref,
                        wa_ref, wb_ref, wc_ref, wo_ref)


def _merge_sample(x, ya, yb, yc, gate, wa, wb, wc, wo, *, layer):
    m = x.shape[0]
    acts, weights = (x, ya, yb, yc, gate), (wa, wb, wc, wo)
    args = acts + weights
    blocks = (sum(_nbytes(a.shape, a.dtype) for a in acts) + _nbytes(x.shape, F32)
              + sum(_nbytes(w.shape[1:], w.dtype) for w in weights))
    return pl.pallas_call(
        _merge_kernel,
        grid=(1,),
        in_specs=[_resident(a.shape) for a in acts] + [_layer_resident(w, layer) for w in weights],
        out_specs=pl.BlockSpec((m, D_MODEL), lambda i: (0, 0)),
        out_shape=jax.ShapeDtypeStruct((m, D_MODEL), F32),
        compiler_params=pltpu.CompilerParams(
            dimension_semantics=("arbitrary",), vmem_limit_bytes=_vmem_limit(blocks)),
        name="merge_sample",
    )(*args)


_PAGES_PER_CHUNK = 16
_RING_SLOTS = 4
_RING_AHEAD = _RING_SLOTS - 1


def _moba_sample_steps(pt_ref, q_ref, kn_ref, vnew_ref, ck_hbm, cv_hbm, o_ref,
                       buf, sem, s_ref, *, page_base, n_pages, page_size, n_q):
    b = pl.program_id(0)
    n_req = pl.num_programs(0)
    chunk_keys = _PAGES_PER_CHUNK * page_size
    n_chunks = n_pages // _PAGES_PER_CHUNK
    blocks_per_chunk = chunk_keys // MOBA_BLOCK
    n_past_blk = n_chunks * blocks_per_chunk
    past_len = n_pages * page_size
    q_rows = n_q * MOBA_HEADS

    def copies(req, c, slot):
        src = ck_hbm if c < n_chunks else cv_hbm
        first = req * n_pages + (c % n_chunks) * _PAGES_PER_CHUNK
        return [pltpu.make_async_copy(src.at[pt_ref[first + p] + page_base],
                                      buf.at[slot, :, pl.ds(p * page_size, page_size)],
                                      sem.at[slot])
                for p in range(_PAGES_PER_CHUNK)]

    def start(req, c, slot):
        for cp in copies(req, c, slot):
            cp.start()

    def wait(req, c, slot):
        for cp in copies(req, c, slot):
            cp.wait()

    n_total = 2 * n_chunks

    @pl.when(b == 0)
    def _():
        for c0 in range(_RING_AHEAD):
            start(b, c0, c0 % _RING_SLOTS)

    sub_i = lax.broadcasted_iota(jnp.int32, (SAMPLE_ROWS, MOBA_DIM), 0)
    own_head = sub_i == lax.broadcasted_iota(jnp.int32, (SAMPLE_ROWS, MOBA_DIM), 1) // MOBA_HEAD_DIM
    q8 = q_ref[...] * (MOBA_HEAD_DIM ** -0.5)
    q_exp = jnp.concatenate(
        [jnp.where(own_head, jnp.broadcast_to(q8[t:t + 1, :], (MOBA_HEADS, MOBA_DIM)), 0.0)
         for t in range(n_q)], axis=0)
    q_exp_b = q_exp.astype(BF16)

    def pad_rows(x8):
        return jnp.concatenate([x8, jnp.zeros((LANES - SAMPLE_ROWS, x8.shape[1]), F32)], axis=0)

    blk_lane = lax.broadcasted_iota(jnp.int32, (MOBA_DIM, LANES), 1)
    ksum = jnp.zeros((MOBA_DIM, LANES), F32)
    acc = jnp.zeros((q_rows, MOBA_DIM), F32)
    inv_l = None
    for c in range(n_total):
        slot = c % _RING_SLOTS
        wait(b, c, slot)
        nxt = c + _RING_AHEAD
        if nxt < n_total:
            start(b, nxt, nxt % _RING_SLOTS)
        else:
            @pl.when(b + 1 < n_req)
            def _(nxt=nxt):
                start(b + 1, nxt - n_total, nxt % _RING_SLOTS)

        if c < n_chunks:
            kc = buf[slot]
            for j in range(blocks_per_chunk):
                col = jnp.sum(kc[:, j * MOBA_BLOCK:(j + 1) * MOBA_BLOCK], axis=1, keepdims=True)
                ksum = jnp.where(blk_lane == c * blocks_per_chunk + j, col, ksum)
            s_ref[:, c * chunk_keys:(c + 1) * chunk_keys] = _dot(q_exp_b, kc.astype(BF16))
            if c == n_chunks - 1:
                gate = jnp.dot(q_exp, ksum * (1.0 / MOBA_BLOCK), precision=lax.Precision.HIGHEST,
                               preferred_element_type=F32)
                lane = lax.broadcasted_iota(jnp.int32, gate.shape, 1)
                rank = jnp.zeros(gate.shape, F32)
                for m in range(n_past_blk):
                    gm = gate[:, m:m + 1]
                    beats = (gm > gate) | ((gm == gate) & (lane > m))
                    rank = rank + beats.astype(F32)
                sel = ((lane < n_past_blk) & (rank < MOBA_TOPK)).astype(F32)
                s_own = _dot_nt(q_exp_b, pad_rows(kn_ref[...]).astype(BF16))
                t_i = lax.broadcasted_iota(jnp.int32, s_own.shape, 0) // MOBA_HEADS
                s_own = jnp.where(lane <= t_i, s_own, MASK_VALUE)
                mx_acc = jnp.full((q_rows, MOBA_BLOCK), MASK_VALUE, F32)
                for n in range(n_past_blk):
                    cs = slice(n * MOBA_BLOCK, (n + 1) * MOBA_BLOCK)
                    blk = jnp.where(sel[:, n:n + 1] > 0.5, s_ref[:, cs], MASK_VALUE)
                    s_ref[:, cs] = blk
                    mx_acc = jnp.maximum(mx_acc, blk)
                mx = jnp.maximum(jnp.max(mx_acc, axis=1, keepdims=True),
                                 jnp.max(s_own, axis=1, keepdims=True))
                p_own = jnp.exp(s_own - mx)
                l_acc = jnp.zeros((q_rows, MOBA_BLOCK), F32)
                for n in range(n_past_blk):
                    cs = slice(n * MOBA_BLOCK, (n + 1) * MOBA_BLOCK)
                    blk = jnp.exp(s_ref[:, cs] - mx)
                    s_ref[:, cs] = blk
                    l_acc = l_acc + blk
                s_ref[:, past_len:past_len + LANES] = p_own
                inv_l = 1.0 / (jnp.sum(l_acc, axis=1, keepdims=True)
                               + jnp.sum(p_own, axis=1, keepdims=True))
        else:
            cc = c - n_chunks
            p_c = s_ref[:, cc * chunk_keys:(cc + 1) * chunk_keys].astype(BF16)
            acc = acc + _dot_nt(p_c, buf[slot].astype(BF16))
        yield c

    p_own = s_ref[:, past_len:past_len + LANES].astype(BF16)
    acc = (acc + _dot(p_own, pad_rows(vnew_ref[...]).astype(BF16))) * inv_l
    y = jnp.zeros((SAMPLE_ROWS, MOBA_DIM), F32)
    for t in range(n_q):
        grp = jnp.where(own_head, acc[t * MOBA_HEADS:(t + 1) * MOBA_HEADS, :], 0.0)
        y = jnp.where(sub_i == t, jnp.sum(grp, axis=0, keepdims=True), y)
    o_ref[...] = y


def _ffn_steps(x_ref, g_ref, w1_ref, w3_ref, w2_ref, o_ref, a_ref):
    x = x_ref[...]
    xb = _rms(x, g_ref[...]).astype(BF16)

    def hidden(s, w):
        h1 = _dot(xb, w1_ref[:, s:s + w])
        h3 = _dot(xb, w3_ref[:, s:s + w])
        a_ref[:, s:s + w] = (h1 * _sigmoid(h1) * h3).astype(BF16)

    def out_cols(c0, c1):
        o_ref[:, c0:c1] = x[:, c0:c1] + 0.5 * _dot(a_ref[...], w2_ref[:, c0:c1])

    half = D_MODEL // 2
    return ([functools.partial(hidden, s, w) for s, w in _FF_CHUNKS]
            + [functools.partial(out_cols, 0, half), functools.partial(out_cols, half, D_MODEL)])


def _ffn_moba_kernel(pt_ref, x_ref, g_ref, w1_ref, w3_ref, w2_ref, q_ref, kn_ref, vnew_ref,
                     ck_hbm, cv_hbm, o_ref, yb_ref, a_ref, buf, sem, s_ref, **moba_params):
    ffn = _ffn_steps(x_ref, g_ref, w1_ref, w3_ref, w2_ref, o_ref, a_ref)
    moba = _moba_sample_steps(pt_ref, q_ref, kn_ref, vnew_ref, ck_hbm, cv_hbm, yb_ref,
                              buf, sem, s_ref, **moba_params)
    n_chunk_steps = 2 * moba_params["n_pages"] // _PAGES_PER_CHUNK
    every = max(1, n_chunk_steps // len(ffn))
    for c in moba:
        if (c + 1) % every == 0 and ffn:
            ffn.pop(0)()
    for piece in ffn:
        piece()


def _ffn_moba(x, g, w1, w3, w2, page_table, q, k_new, v_new, cache_k, cache_v, *, layer, tm, n_q):
    m = x.shape[0]
    n_req, n_pages = page_table.shape
    assert m // tm == n_req
    depth, n_pool, page_size = cache_k.shape[:3]
    ck = jnp.transpose(cache_k, (0, 1, 3, 4, 2)).reshape(depth * n_pool, MOBA_DIM, page_size)
    cv = jnp.transpose(cache_v, (0, 1, 3, 4, 2)).reshape(depth * n_pool, MOBA_DIM, page_size)
    chunk_keys = _PAGES_PER_CHUNK * page_size
    past_len = n_pages * page_size
    q_rows = n_q * MOBA_HEADS
    assert (2 * n_pages // _PAGES_PER_CHUNK) % _RING_SLOTS == 0
    tile = pl.BlockSpec((tm, D_MODEL), lambda i, pt: (i, 0))
    rows = pl.BlockSpec((SAMPLE_ROWS, MOBA_DIM), lambda i, pt: (i, 0))
    hbm = pl.BlockSpec(memory_space=pl.ANY)
    scratch = [pltpu.VMEM((tm, D_FF), BF16),
               pltpu.VMEM((_RING_SLOTS, MOBA_DIM, chunk_keys), F32),
               pltpu.SemaphoreType.DMA((_RING_SLOTS,)),
               pltpu.VMEM((q_rows, past_len + LANES), F32)]
    blocks = (2 * 2 * _nbytes((tm, D_MODEL), F32) + 3 * _nbytes((D_MODEL, D_FF), BF16)
              + _nbytes((tm, D_FF), BF16)
              + _nbytes((_RING_SLOTS, MOBA_DIM, chunk_keys), F32)
              + _nbytes((q_rows, past_len + LANES), F32) + 8 * _nbytes((SAMPLE_ROWS, MOBA_DIM), F32))
    return pl.pallas_call(
        functools.partial(_ffn_moba_kernel, page_base=layer * n_pool, n_pages=n_pages,
                          page_size=page_size, n_q=n_q),
        grid_spec=pltpu.PrefetchScalarGridSpec(
            num_scalar_prefetch=1,
            grid=(n_req,),
            in_specs=[tile, _resident((1, D_MODEL)), _layer_resident(w1, layer),
                      _layer_resident(w3, layer), _layer_resident(w2, layer),
                      rows, rows, rows, hbm, hbm],
            out_specs=[tile, rows],
            scratch_shapes=scratch),
        out_shape=[jax.ShapeDtypeStruct((m, D_MODEL), F32),
                   jax.ShapeDtypeStruct((n_req * SAMPLE_ROWS, MOBA_DIM), F32)],
        compiler_params=pltpu.CompilerParams(
            dimension_semantics=("arbitrary",), vmem_limit_bytes=_vmem_limit(blocks)),
        name="ffn_moba",
    )(page_table.reshape(-1), x, g, w1, w3, w2, q, k_new, v_new, ck, cv)


def _sample_sgu_tables(sgu_w, sgu_b, n_b, t_new):
    t = jnp.arange(SAMPLE_ROWS)
    d = jnp.arange(t_new)
    src = t[None, :] - d[:, None]
    ok = (src >= 0) & (t[None, :] < t_new)
    w_td = sgu_w[:, t[None, :].clip(0, t_new - 1), src.clip(0, t_new - 1)]
    w_td = jnp.where(ok[None], w_td, 0.0)
    wd = jnp.repeat(jnp.transpose(w_td, (1, 2, 0)), SGU_GROUP_DIM, axis=2)
    bl = jnp.where((t < t_new)[:, None],
                   jnp.repeat(sgu_b[:, t.clip(0, t_new - 1)].T, SGU_GROUP_DIM, axis=1), 0.0)
    return jnp.tile(wd, (1, n_b, 1)), jnp.tile(bl, (n_b, 1))


def kernel(x_prompt, x_sample, mem_prompt, cache_k, cache_v, cache_mem_k, cache_mem_v, page_table,
           ffn1_norm, ffn1_w1, ffn1_w3, ffn1_w2, mix_norm, w_in, sgu_ln_g, sgu_ln_b, sgu_w, sgu_b,
           mem_norm, w_mem_kv, w_a_out, w_b_out, w_c_out, w_o, ffn2_norm, ffn2_w1, ffn2_w3, ffn2_w2,
           final_norm):
    depth = w_in.shape[0]
    bsz, t_len, _ = x_prompt.shape
    dec_b, t_new, _ = x_sample.shape
    n_pages = page_table.shape[1]
    page_size = cache_k.shape[2]
    past_len = n_pages * page_size
    mem_len = mem_prompt.shape[1]
    assert t_len % 512 == 0 and t_new <= SAMPLE_ROWS and past_len % MOBA_BLOCK == 0
    assert n_pages % _PAGES_PER_CHUNK == 0 and (_PAGES_PER_CHUNK * page_size) % MOBA_BLOCK == 0

    row = lambda a: a.reshape(1, -1)
    fin = row(final_norm)
    f1 = tuple(w.astype(BF16) for w in (ffn1_w1, ffn1_w3, ffn1_w2))
    f2 = tuple(w.astype(BF16) for w in (ffn2_w1, ffn2_w3, ffn2_w2))
    win, wkv = w_in.astype(BF16), w_mem_kv.astype(BF16)
    outs_w = tuple(w.astype(BF16) for w in (w_a_out, w_b_out, w_c_out, w_o))

    tm_p = 512
    cos_p, slo_p, shi_p = _rope_tables(jnp.arange(t_len, dtype=jnp.int32))
    hp = x_prompt.reshape(bsz * t_len, D_MODEL)
    mem = mem_prompt.reshape(bsz * mem_len, D_MODEL)
    kv_t, mkp, mvp = (), [], []

    m_s = dec_b * SAMPLE_ROWS
    pos_s = past_len + jnp.arange(SAMPLE_ROWS, dtype=jnp.int32)
    cos_s, slo_s, shi_s = (jnp.tile(t, (dec_b, 1)) for t in _rope_tables(pos_s))
    hs = jnp.pad(x_sample, ((0, 0), (0, SAMPLE_ROWS - t_new), (0, 0))).reshape(m_s, D_MODEL)
    n_b = 8
    mem_k = cache_mem_k.reshape(depth, dec_b, mem_len * MEM_HEADS, MEM_HEAD_DIM)
    mem_v = cache_mem_v.reshape(depth, dec_b, mem_len * MEM_HEADS, MEM_HEAD_DIM)
    ks_, vs_, sv_ = [], [], []
    unpad = lambda a: a.reshape(dec_b, SAMPLE_ROWS, -1)[:, :t_new]

    for l in range(depth):
        wd, bl = _sample_sgu_tables(sgu_w[l], sgu_b[l], n_b, t_new)
        hs = _ffn(hs, row(ffn1_norm[l]), *f1, fin, layer=l, tm=m_s, final_norm=False)
        u_s, vn_s, q_s, k_s, v_s, cq_s, gate_s = _inproj(
            hs, row(mix_norm[l]), win, row(sgu_ln_g[l]), row(sgu_ln_b[l]), cos_s, slo_s, shi_s,
            layer=l, tm=m_s, t_major_kv=False, prev_kv_t=(), act_dtype=F32)
        ya_s, yc_s = _premix_sample(u_s, vn_s, cq_s, mem_k, mem_v, wd, bl, layer=l, n_b=n_b)

        mk, mv = _memkv(mem, row(mem_norm[l]), wkv, layer=l, tm=512)
        hp, yb_s = _ffn_moba(hp, row(ffn1_norm[l]), *f1, page_table, q_s, k_s, v_s, cache_k, cache_v,
                             layer=l, tm=tm_p, n_q=t_new)
        u, vn, q, k, v, cq, gate, *kv_t = _inproj(
            hp, row(mix_norm[l]), win, row(sgu_ln_g[l]), row(sgu_ln_b[l]), cos_p, slo_p, shi_p,
            layer=l, tm=tm_p, t_major_kv=True, prev_kv_t=kv_t, act_dtype=BF16)
        yb = _moba_prompt(q.reshape(bsz, t_len, MOBA_DIM), k.reshape(bsz, t_len, MOBA_DIM),
                          v.reshape(bsz, t_len, MOBA_DIM), out_dtype=BF16).reshape(bsz * t_len, MOBA_DIM)
        hp = _mix_prompt(u, vn, cq, gate, yb, hp, mk.reshape(bsz, mem_len, MEM_DIM),
                         mv.reshape(bsz, mem_len, MEM_DIM), sgu_w, sgu_b[l].T, *outs_w,
                         layer=l, tm=tm_p, t_len=t_len)
        hp = _ffn(hp, row(ffn2_norm[l]), *f2, fin, layer=l, tm=2 * tm_p, final_norm=(l == depth - 1))
        mkp.append(mk.reshape(bsz, mem_len, MEM_HEADS, MEM_HEAD_DIM))
        mvp.append(mv.reshape(bsz, mem_len, MEM_HEADS, MEM_HEAD_DIM))

        hs = _merge_sample(hs, ya_s, yb_s, yc_s, gate_s, *outs_w, layer=l)
        hs = _ffn(hs, row(ffn2_norm[l]), *f2, fin, layer=l, tm=m_s, final_norm=(l == depth - 1))
        ks_.append(unpad(k_s).reshape(dec_b, t_new, MOBA_HEADS, MOBA_HEAD_DIM))
        vs_.append(unpad(v_s).reshape(dec_b, t_new, MOBA_HEADS, MOBA_HEAD_DIM))
        sv_.append(unpad(vn_s))
    y_prompt = hp.reshape(bsz, t_len, D_MODEL)
    y_sample = unpad(hs)

    def token_minor_to_rows(stack):
        return jnp.transpose(stack.reshape(depth, bsz, MOBA_HEADS, MOBA_HEAD_DIM, t_len), (0, 1, 4, 2, 3))

    return (y_prompt, y_sample, token_minor_to_rows(kv_t[0]), token_minor_to_rows(kv_t[1]),
            jnp.stack(mkp), jnp.stack(mvp), jnp.stack(ks_), jnp.stack(vs_), jnp.stack(sv_))
```

```python
import functools

import jax
import jax.numpy as jnp
import numpy as np
from jax import lax
from jax.experimental import pallas as pl
from jax.experimental.pallas import tpu as pltpu

F32 = jnp.float32
BF16 = jnp.bfloat16

D_MODEL = 1024
SGU_GROUPS = 4
SGU_GROUP_DIM = 128
SGU_DIM = 512
CHUNK = 128
MOBA_HEADS = 8
MOBA_HEAD_DIM = 64
MOBA_DIM = 512
MOBA_BLOCK = 256
MOBA_TOPK = 3
MEM_HEADS = 4
MEM_HEAD_DIM = 128
MEM_DIM = 512
D_FF = 2816
ROPE_THETA = 10000.0
NORM_EPS = 1e-6
MASK_VALUE = -1e30
LOG2_E = float(np.log2(np.e))
IN_COLS = 6144
GATE_COL0 = 3072

V7X_VMEM_BYTES = 64 * 1024 * 1024
V7X_VMEM_RESERVE = 4 * 1024 * 1024
LANES = 128
SUBLANES = 8
SAMPLE_ROWS = SUBLANES
TEMP_ALLOWANCE = 12 * 1024 * 1024

NT_DIMS = (((1,), (1,)), ((), ()))


def _vmem_limit(block_bytes):
    return int(min(block_bytes + TEMP_ALLOWANCE, V7X_VMEM_BYTES - V7X_VMEM_RESERVE))


def _nbytes(shape, dtype):
    return int(np.prod(shape)) * jnp.dtype(dtype).itemsize


def _resident(shape):
    nd = len(shape)
    return pl.BlockSpec(shape, lambda *_: (0,) * nd, pipeline_mode=pl.Buffered(1))


def _layer_resident(arr, layer):
    nd = arr.ndim
    return pl.BlockSpec((None,) + arr.shape[1:], lambda *_: (layer,) + (0,) * (nd - 1),
                        pipeline_mode=pl.Buffered(1))


def _dot(a, b):
    return jnp.dot(a, b, preferred_element_type=F32)


def _dot_nt(a, b, precision=None):
    return lax.dot_general(a, b, NT_DIMS, precision=precision, preferred_element_type=F32)


def _rms(x, g):
    return x * lax.rsqrt(jnp.mean(x * x, axis=-1, keepdims=True) + NORM_EPS) * g


def _sigmoid(x):
    return 1.0 / (1.0 + jnp.exp(-x))


def _gelu(x):
    return 0.5 * x * (1.0 + lax.erf(x * np.float32(np.sqrt(0.5))))


def _softmax_rows(s):
    p = jnp.exp(s - jnp.max(s, axis=-1, keepdims=True))
    return p, 1.0 / jnp.sum(p, axis=-1, keepdims=True)


_FF_CHUNK = 512
_FF_CHUNKS = tuple((s, min(_FF_CHUNK, D_FF - s)) for s in range(0, D_FF, _FF_CHUNK))


def _ffn_kernel(x_ref, g_ref, w1_ref, w3_ref, w2_ref, fg_ref, o_ref, a_ref, *, final_norm):
    x = x_ref[...]
    xb = _rms(x, g_ref[...]).astype(BF16)
    for s, w in _FF_CHUNKS:
        h1 = _dot(xb, w1_ref[:, s:s + w])
        h3 = _dot(xb, w3_ref[:, s:s + w])
        a_ref[:, s:s + w] = (h1 * _sigmoid(h1) * h3).astype(BF16)
    y = x + 0.5 * _dot(a_ref[...], w2_ref[...])
    if final_norm:
        y = _rms(y, fg_ref[...])
    o_ref[...] = y


def _ffn(x, g, w1, w3, w2, fg, *, layer, tm, final_norm):
    m = x.shape[0]
    blocks = (2 * 2 * _nbytes((tm, D_MODEL), F32) + 3 * _nbytes((D_MODEL, D_FF), BF16)
              + _nbytes((tm, D_FF), BF16))
    return pl.pallas_call(
        functools.partial(_ffn_kernel, final_norm=final_norm),
        grid=(m // tm,),
        in_specs=[
            pl.BlockSpec((tm, D_MODEL), lambda i: (i, 0)),
            _resident((1, D_MODEL)),
            _layer_resident(w1, layer),
            _layer_resident(w3, layer),
            _layer_resident(w2, layer),
            _resident((1, D_MODEL)),
        ],
        out_specs=pl.BlockSpec((tm, D_MODEL), lambda i: (i, 0)),
        out_shape=jax.ShapeDtypeStruct((m, D_MODEL), F32),
        scratch_shapes=[pltpu.VMEM((tm, D_FF), BF16)],
        compiler_params=pltpu.CompilerParams(
            dimension_semantics=("parallel",), vmem_limit_bytes=_vmem_limit(blocks)),
        name="ffn",
    )(x, g, w1, w3, w2, fg)


def _rope(h, cos, sin_lo, sin_hi):
    outs = []
    for c in range(h.shape[1] // LANES):
        s = h[:, c * LANES:(c + 1) * LANES]
        outs.append(s * cos + pltpu.roll(s, LANES - 32, 1) * sin_lo + pltpu.roll(s, 32, 1) * sin_hi)
    return jnp.concatenate(outs, axis=1)


def _inproj_kernel(x_ref, g_ref, w_ref, lng_ref, lnb_ref, cos_ref, slo_ref, shi_ref, *refs, n_prev):
    prev_refs, (u_ref, vn_ref, q_ref, k_ref, v_ref, cq_ref, gate_ref), t_refs = (
        refs[:n_prev], refs[n_prev:n_prev + 7], refs[n_prev + 7:])
    xb = _rms(x_ref[...], g_ref[...]).astype(BF16)

    def proj(c0, w=512):
        return _dot(xb, w_ref[:, c0:c0 + w])

    u_ref[...] = _gelu(proj(0)).astype(u_ref.dtype)
    a = _gelu(proj(512))
    ac = a - jnp.mean(a, axis=-1, keepdims=True)
    vn_ref[...] = (ac * lax.rsqrt(jnp.mean(ac * ac, axis=-1, keepdims=True) + NORM_EPS)
                   * lng_ref[...] + lnb_ref[...]).astype(vn_ref.dtype)
    cos, slo, shi = cos_ref[...], slo_ref[...], shi_ref[...]
    q_ref[...] = _rope(proj(1024), cos, slo, shi)
    k = _rope(proj(1536), cos, slo, shi)
    v = proj(2048)
    k_ref[...] = k
    v_ref[...] = v
    for t_ref, new, prev_ref in zip(t_refs, (k, v), prev_refs or (None, None)):
        n_before = t_ref.shape[0] - 1
        if n_before:
            t_ref[:n_before] = prev_ref[...]
        t_ref[n_before] = new.T
    cq_ref[...] = proj(2560).astype(cq_ref.dtype)
    for c in range(6):
        gate_ref[:, c * 512:(c + 1) * 512] = _sigmoid(proj(GATE_COL0 + c * 512)).astype(gate_ref.dtype)


def _inproj(x, g, w, lng, lnb, cos, slo, shi, *, layer, tm, t_major_kv, prev_kv_t, act_dtype):
    m = x.shape[0]
    t_len = cos.shape[0]
    n_tab = t_len // tm
    out_cols = (512,) * 6 + (3 * D_MODEL,)
    blocks = (2 * _nbytes((tm, D_MODEL), F32) + _nbytes((D_MODEL, IN_COLS), BF16)
              + 2 * _nbytes((tm, IN_COLS), F32) + 6 * _nbytes((tm, LANES), F32))
    tab = pl.BlockSpec((tm, LANES), lambda i: (i % n_tab, 0))
    out_specs = [pl.BlockSpec((tm, c), lambda i: (i, 0)) for c in out_cols]
    out_dtypes = (act_dtype, act_dtype, F32, F32, F32, act_dtype, act_dtype)
    out_shape = [jax.ShapeDtypeStruct((m, c), dt) for c, dt in zip(out_cols, out_dtypes)]
    prev_specs = []
    if t_major_kv:
        def stack_spec(n):
            return pl.BlockSpec((n, None, MOBA_DIM, tm), lambda i: (0, i // n_tab, 0, i % n_tab))

        assert len(prev_kv_t) == (2 if layer else 0)
        blocks += 4 * (2 * layer + 1) * _nbytes((MOBA_DIM, tm), F32)
        prev_specs = [stack_spec(layer)] * len(prev_kv_t)
        out_specs += [stack_spec(layer + 1)] * 2
        out_shape += [jax.ShapeDtypeStruct((layer + 1, m // t_len, MOBA_DIM, t_len), F32)] * 2
    return pl.pallas_call(
        functools.partial(_inproj_kernel, n_prev=len(prev_specs)),
        grid=(m // tm,),
        in_specs=[
            pl.BlockSpec((tm, D_MODEL), lambda i: (i, 0)),
            _resident((1, D_MODEL)),
            _layer_resident(w, layer),
            _resident((1, SGU_DIM)),
            _resident((1, SGU_DIM)),
            tab, tab, tab,
        ] + prev_specs,
        out_specs=out_specs,
        out_shape=out_shape,
        compiler_params=pltpu.CompilerParams(
            dimension_semantics=("parallel",), vmem_limit_bytes=_vmem_limit(blocks)),
        name="inproj",
    )(x, g, w, lng, lnb, cos, slo, shi, *prev_kv_t)


def _rope_tables(pos):
    half = MOBA_HEAD_DIM // 2
    inv = ROPE_THETA ** (-jnp.arange(half, dtype=F32) * 2.0 / MOBA_HEAD_DIM)
    ang = pos.astype(F32)[:, None] * inv[None, :]
    cos, sin = jnp.tile(jnp.cos(ang), (1, 4)), jnp.tile(jnp.sin(ang), (1, 4))
    first_half = (jnp.arange(LANES) % MOBA_HEAD_DIM) < half
    return cos, jnp.where(first_half, -sin, 0.0), jnp.where(first_half, 0.0, sin)


def _memkv_kernel(m_ref, g_ref, w_ref, k_ref, v_ref):
    xb = _rms(m_ref[...], g_ref[...]).astype(BF16)
    k_ref[...] = _dot(xb, w_ref[:, :MEM_DIM])
    v_ref[...] = _dot(xb, w_ref[:, MEM_DIM:])


def _memkv(mem, g, w, *, layer, tm):
    m = mem.shape[0]
    blocks = (2 * _nbytes((tm, D_MODEL), F32) + _nbytes((D_MODEL, 2 * MEM_DIM), BF16)
              + 4 * _nbytes((tm, MEM_DIM), F32))
    return pl.pallas_call(
        _memkv_kernel,
        grid=(m // tm,),
        in_specs=[pl.BlockSpec((tm, D_MODEL), lambda i: (i, 0)), _resident((1, D_MODEL)),
                  _layer_resident(w, layer)],
        out_specs=[pl.BlockSpec((tm, MEM_DIM), lambda i: (i, 0))] * 2,
        out_shape=[jax.ShapeDtypeStruct((m, MEM_DIM), F32)] * 2,
        compiler_params=pltpu.CompilerParams(
            dimension_semantics=("parallel",), vmem_limit_bytes=_vmem_limit(blocks)),
        name="memkv",
    )(mem, g, w)


def _topk_select(g, n_past):
    row = lax.broadcasted_iota(jnp.int32, g.shape, 0)
    rank = jnp.zeros(g.shape, F32)
    for m in range(n_past):
        gm = g[m:m + 1, :]
        beats = (gm > g) | ((gm == g) & (row > m))
        rank = rank + beats.astype(F32)
    return (row < n_past) & (rank < MOBA_TOPK)


_MOBA_SLABS = 2


def _moba_prompt_kernel(q_ref, k_ref, v_ref, o_ref):
    t_len = q_ref.shape[1]
    n_blk = t_len // MOBA_BLOCK
    n_slabs = q_ref.shape[2] // LANES
    lane = lax.broadcasted_iota(jnp.int32, (1, LANES), 1)
    head_masks = (lane < MOBA_HEAD_DIM, lane >= MOBA_HEAD_DIM)
    r_i = lax.broadcasted_iota(jnp.int32, (2 * MOBA_BLOCK, MOBA_BLOCK), 0) & (MOBA_BLOCK - 1)
    c_i = lax.broadcasted_iota(jnp.int32, (2 * MOBA_BLOCK, MOBA_BLOCK), 1)
    causal2 = c_i <= r_i
    scale = MOBA_HEAD_DIM ** -0.5 * LOG2_E
    ones = jnp.ones((t_len, LANES), BF16)

    kb, v_aug, k_mean = [], [], []
    for sl in range(n_slabs):
        k = k_ref[0, :, sl * LANES:(sl + 1) * LANES]
        kb.append(k.astype(BF16))
        v_aug.append(jnp.concatenate(
            [v_ref[0, :, sl * LANES:(sl + 1) * LANES].astype(BF16), ones], axis=1))
        k_mean.append(jnp.sum(k.reshape(n_blk, MOBA_BLOCK, LANES), axis=1) * (1.0 / MOBA_BLOCK))

    def scores(sl, qi):
        q_blk = q_ref[0, qi * MOBA_BLOCK:(qi + 1) * MOBA_BLOCK, sl * LANES:(sl + 1) * LANES]
        q2 = jnp.concatenate([jnp.where(hm, q_blk, 0.0) for hm in head_masks], axis=0)
        return q2, _dot_nt((q2 * scale).astype(BF16), kb[sl][:(qi + 1) * MOBA_BLOCK])

    def probabilities(sl, qi, q2, s):
        parts = []
        if qi > MOBA_TOPK:
            gate = _dot_nt(k_mean[sl], q2, precision=lax.Precision.HIGHEST)
            sel = _topk_select(gate, qi).astype(F32)
            sel_t = jnp.concatenate(
                [sel, jnp.zeros((LANES - n_blk, 2 * MOBA_BLOCK), F32)], axis=0).T
            for n in range(qi):
                keep = sel_t[:, n:n + 1] > 0.5
                parts.append(jnp.where(keep, s[:, n * MOBA_BLOCK:(n + 1) * MOBA_BLOCK], MASK_VALUE))
        else:
            for n in range(qi):
                parts.append(s[:, n * MOBA_BLOCK:(n + 1) * MOBA_BLOCK])
        parts.append(jnp.where(causal2, s[:, qi * MOBA_BLOCK:], MASK_VALUE))
        sm = jnp.concatenate(parts, axis=1) if len(parts) > 1 else parts[0]
        return jnp.exp2(sm - jnp.max(sm, axis=-1, keepdims=True)).astype(BF16)

    def weighted_values(sl, qi, p):
        o2 = _dot(p, v_aug[sl][:(qi + 1) * MOBA_BLOCK])
        o2 = o2[:, :LANES] * (1.0 / o2[:, LANES:])
        o_ref[0, qi * MOBA_BLOCK:(qi + 1) * MOBA_BLOCK, sl * LANES:(sl + 1) * LANES] = jnp.where(
            head_masks[0], o2[:MOBA_BLOCK], o2[MOBA_BLOCK:]).astype(o_ref.dtype)

    slabs = range(n_slabs)
    ahead = [scores(sl, 0) for sl in slabs]
    pending = None
    for qi in range(n_blk):
        cur = ahead
        if qi + 1 < n_blk:
            ahead = [scores(sl, qi + 1) for sl in slabs]
        probs = [probabilities(sl, qi, *cur[sl]) for sl in slabs]
        if pending is not None:
            for sl in slabs:
                weighted_values(sl, qi - 1, pending[sl])
        pending = probs
    for sl in slabs:
        weighted_values(sl, n_blk - 1, pending[sl])


def _moba_prompt(q, k, v, *, out_dtype):
    bsz, t_len, _ = q.shape
    width = _MOBA_SLABS * LANES
    spec = pl.BlockSpec((1, t_len, width), lambda b, h: (b, 0, h))
    blocks = _MOBA_SLABS * (8 * _nbytes((t_len, LANES), F32) + 3 * _nbytes((t_len, LANES), BF16)
                            + 8 * _nbytes((MOBA_BLOCK, t_len), F32))
    return pl.pallas_call(
        _moba_prompt_kernel,
        grid=(bsz, MOBA_DIM // width),
        in_specs=[spec, spec, spec],
        out_specs=spec,
        out_shape=jax.ShapeDtypeStruct((bsz, t_len, MOBA_DIM), out_dtype),
        compiler_params=pltpu.CompilerParams(
            dimension_semantics=("parallel", "parallel"), vmem_limit_bytes=_vmem_limit(blocks)),
        name="moba_prompt",
    )(q, k, v)


def _merge(x, ya, yb, yc, gate_ref, wa_ref, wb_ref, wc_ref, wo_ref):
    t = (gate_ref[:, 0:D_MODEL] * _dot(ya.astype(BF16), wa_ref[...])
         + gate_ref[:, D_MODEL:2 * D_MODEL] * _dot(yb.astype(BF16), wb_ref[...])
         + gate_ref[:, 2 * D_MODEL:3 * D_MODEL] * _dot(yc.astype(BF16), wc_ref[...]))
    return x + _dot(t.astype(BF16), wo_ref[...])


def _mem_attention(cq, mkb, mvb):
    scale = MEM_HEAD_DIM ** -0.5
    outs = []
    for h in range(MEM_HEADS):
        sl = slice(h * MEM_HEAD_DIM, (h + 1) * MEM_HEAD_DIM)
        s = _dot_nt(cq[:, sl].astype(BF16), mkb[:, sl]) * scale
        p, inv_l = _softmax_rows(s)
        outs.append(_dot(p.astype(BF16), mvb[:, sl]) * inv_l)
    return jnp.concatenate(outs, axis=1)


def _mix_prompt_kernel(u_ref, vn_ref, cq_ref, gate_ref, yb_ref, x_ref, mk_ref, mv_ref,
                       sw_ref, sb_ref, wa_ref, wb_ref, wc_ref, wo_ref, o_ref):
    tm = u_ref.shape[0]
    r_i = lax.broadcasted_iota(jnp.int32, (CHUNK, CHUNK), 0)
    c_i = lax.broadcasted_iota(jnp.int32, (CHUNK, CHUNK), 1)
    wm = [jnp.where(c_i <= r_i, sw_ref[g], 0.0).astype(BF16) for g in range(SGU_GROUPS)]
    rows = []
    for ch in range(tm // CHUNK):
        rs = slice(ch * CHUNK, (ch + 1) * CHUNK)
        cols = []
        for g in range(SGU_GROUPS):
            cs = slice(g * SGU_GROUP_DIM, (g + 1) * SGU_GROUP_DIM)
            y = _dot(wm[g], vn_ref[rs, cs].astype(BF16)) + sb_ref[:, g:g + 1]
            cols.append(u_ref[rs, cs].astype(F32) * y)
        rows.append(jnp.concatenate(cols, axis=1))
    ya = jnp.concatenate(rows, axis=0)
    yc = _mem_attention(cq_ref[...], mk_ref[0].astype(BF16), mv_ref[0].astype(BF16))
    o_ref[...] = _merge(x_ref[...], ya, yb_ref[...], yc, gate_ref, wa_ref, wb_ref, wc_ref, wo_ref)


def _mix_prompt(u, vn, cq, gate, yb, x, mk, mv, sw, sb_t, wa, wb, wc, wo, *, layer, tm, t_len):
    m = x.shape[0]
    per_b = t_len // tm

    def rows(c):
        return pl.BlockSpec((tm, c), lambda i: (i, 0))

    mem_spec = pl.BlockSpec((1,) + mk.shape[1:], lambda i: (i // per_b, 0, 0))
    blocks = (2 * _nbytes((tm, 4 * 512 + 3 * D_MODEL + 2 * D_MODEL), F32)
              + 4 * _nbytes(mk.shape[1:], F32) + _nbytes((3 * 512 + D_MODEL, D_MODEL), BF16))
    return pl.pallas_call(
        _mix_prompt_kernel,
        grid=(m // tm,),
        in_specs=[rows(512), rows(512), rows(512), rows(3 * D_MODEL), rows(512), rows(D_MODEL),
                  mem_spec, mem_spec,
                  _layer_resident(sw, layer), _resident(sb_t.shape),
                  _layer_resident(wa, layer), _layer_resident(wb, layer),
                  _layer_resident(wc, layer), _layer_resident(wo, layer)],
        out_specs=rows(D_MODEL),
        out_shape=jax.ShapeDtypeStruct((m, D_MODEL), F32),
        compiler_params=pltpu.CompilerParams(
            dimension_semantics=("parallel",), vmem_limit_bytes=_vmem_limit(blocks)),
        name="mix_prompt",
    )(u, vn, cq, gate, yb, x, mk, mv, sw, sb_t, wa, wb, wc, wo)


def _premix_sample_kernel(u_ref, vn_ref, cq_ref, mk_ref, mv_ref, wd_ref, bl_ref, ya_ref, yc_ref):
    vn = vn_ref[...]
    y = bl_ref[...] + wd_ref[0] * vn
    for d in range(1, wd_ref.shape[0]):
        y = y + wd_ref[d] * pltpu.roll(vn, d, 0)
    ya_ref[...] = u_ref[...] * y

    q_rows = MEM_HEADS * SAMPLE_ROWS
    n_keys = mk_ref.shape[1]
    row_head = lax.broadcasted_iota(jnp.int32, (q_rows, n_keys), 0) // SAMPLE_ROWS
    key_head = lax.broadcasted_iota(jnp.int32, (q_rows, n_keys), 1) % MEM_HEADS
    same_head = row_head == key_head
    scale = MEM_HEAD_DIM ** -0.5
    for b in range(mk_ref.shape[0]):
        rs = slice(b * SAMPLE_ROWS, (b + 1) * SAMPLE_ROWS)
        q8 = cq_ref[rs, :]
        q_h = jnp.concatenate([q8[:, h * MEM_HEAD_DIM:(h + 1) * MEM_HEAD_DIM]
                               for h in range(MEM_HEADS)], axis=0)
        s = _dot_nt(q_h.astype(BF16), mk_ref[b].astype(BF16)) * scale
        p, inv_l = _softmax_rows(jnp.where(same_head, s, MASK_VALUE))
        o = _dot(p.astype(BF16), mv_ref[b].astype(BF16)) * inv_l
        yc_ref[rs, :] = jnp.concatenate([o[h * SAMPLE_ROWS:(h + 1) * SAMPLE_ROWS, :]
                                         for h in range(MEM_HEADS)], axis=1)


def _premix_sample(u, vn, cq, mk, mv, wd, bl, *, layer, n_b):
    m = u.shape[0]
    tm = n_b * SAMPLE_ROWS
    rows = pl.BlockSpec((tm, 512), lambda i: (i, 0))
    mem_block = (n_b,) + mk.shape[2:]
    mem_spec = pl.BlockSpec((None,) + mem_block, lambda i: (layer, i, 0, 0))
    blocks = (2 * 5 * _nbytes((tm, 512), F32) + 4 * _nbytes(mem_block, F32)
              + _nbytes(wd.shape, F32) + _nbytes(bl.shape, F32))
    return pl.pallas_call(
        _premix_sample_kernel,
        grid=(m // tm,),
        in_specs=[rows, rows, rows, mem_spec, mem_spec, _resident(wd.shape), _resident(bl.shape)],
        out_specs=[rows, rows],
        out_shape=[jax.ShapeDtypeStruct((m, 512), F32)] * 2,
        compiler_params=pltpu.CompilerParams(
            dimension_semantics=("parallel",), vmem_limit_bytes=_vmem_limit(blocks)),
        name="premix_sample",
    )(u, vn, cq, mk, mv, wd, bl)


def _merge_kernel(x_ref, ya_ref, yb_ref, yc_ref, gate_ref, wa_ref, wb_ref, wc_ref, wo_ref, o_ref):
    o_ref[...] = _merge(x_ref[...], ya_ref[...], yb_ref[...], yc_ref[...], gate_ref,
                        wa_ref, wb_ref, wc_ref, wo_ref)


def _merge_sample(x, ya, yb, yc, gate, wa, wb, wc, wo, *, layer):
    m = x.shape[0]
    acts, weights = (x, ya, yb, yc, gate), (wa, wb, wc, wo)
    args = acts + weights
    blocks = (sum(_nbytes(a.shape, a.dtype) for a in acts) + _nbytes(x.shape, F32)
              + sum(_nbytes(w.shape[1:], w.dtype) for w in weights))
    return pl.pallas_call(
        _merge_kernel,
        grid=(1,),
        in_specs=[_resident(a.shape) for a in acts] + [_layer_resident(w, layer) for w in weights],
        out_specs=pl.BlockSpec((m, D_MODEL), lambda i: (0, 0)),
        out_shape=jax.ShapeDtypeStruct((m, D_MODEL), F32),
        compiler_params=pltpu.CompilerParams(
            dimension_semantics=("arbitrary",), vmem_limit_bytes=_vmem_limit(blocks)),
        name="merge_sample",
    )(*args)


_PAGES_PER_CHUNK = 16
_RING_SLOTS = 4
_RING_AHEAD = _RING_SLOTS - 1


def _moba_sample_steps(pt_ref, q_ref, kn_ref, vnew_ref, ck_hbm, cv_hbm, o_ref,
                       buf, sem, s_ref, *, page_base, n_pages, page_size, n_q):
    b = pl.program_id(0)
    n_req = pl.num_programs(0)
    chunk_keys = _PAGES_PER_CHUNK * page_size
    n_chunks = n_pages // _PAGES_PER_CHUNK
    blocks_per_chunk = chunk_keys // MOBA_BLOCK
    n_past_blk = n_chunks * blocks_per_chunk
    past_len = n_pages * page_size
    q_rows = n_q * MOBA_HEADS

    def copies(req, c, slot):
        src = ck_hbm if c < n_chunks else cv_hbm
        first = req * n_pages + (c % n_chunks) * _PAGES_PER_CHUNK
        return [pltpu.make_async_copy(src.at[pt_ref[first + p] + page_base],
                                      buf.at[slot, :, pl.ds(p * page_size, page_size)],
                                      sem.at[slot])
                for p in range(_PAGES_PER_CHUNK)]

    def start(req, c, slot):
        for cp in copies(req, c, slot):
            cp.start()

    def wait(req, c, slot):
        for cp in copies(req, c, slot):
            cp.wait()

    n_total = 2 * n_chunks

    @pl.when(b == 0)
    def _():
        for c0 in range(_RING_AHEAD):
            start(b, c0, c0 % _RING_SLOTS)

    sub_i = lax.broadcasted_iota(jnp.int32, (SAMPLE_ROWS, MOBA_DIM), 0)
    own_head = sub_i == lax.broadcasted_iota(jnp.int32, (SAMPLE_ROWS, MOBA_DIM), 1) // MOBA_HEAD_DIM
    q8 = q_ref[...] * (MOBA_HEAD_DIM ** -0.5)
    q_exp = jnp.concatenate(
        [jnp.where(own_head, jnp.broadcast_to(q8[t:t + 1, :], (MOBA_HEADS, MOBA_DIM)), 0.0)
         for t in range(n_q)], axis=0)
    q_exp_b = q_exp.astype(BF16)

    def pad_rows(x8):
        return jnp.concatenate([x8, jnp.zeros((LANES - SAMPLE_ROWS, x8.shape[1]), F32)], axis=0)

    acc = jnp.zeros((q_rows, MOBA_DIM), F32)
    inv_l = None
    for c in range(n_total):
        slot = c % _RING_SLOTS
        wait(b, c, slot)
        nxt = c + _RING_AHEAD
        if nxt < n_total:
            start(b, nxt, nxt % _RING_SLOTS)
        else:
            @pl.when(b + 1 < n_req)
            def _(nxt=nxt):
                start(b + 1, nxt - n_total, nxt % _RING_SLOTS)

        if c < n_chunks:
            s_ref[:, c * chunk_keys:(c + 1) * chunk_keys] = _dot(q_exp_b, buf[slot].astype(BF16))
            if c == n_chunks - 1:
                lane = lax.broadcasted_iota(jnp.int32, (q_rows, LANES), 1)
                gate = jnp.zeros((q_rows, LANES), F32)
                for n in range(n_past_blk):
                    blk_sum = jnp.sum(s_ref[:, n * MOBA_BLOCK:(n + 1) * MOBA_BLOCK], axis=1, keepdims=True)
                    gate = jnp.where(lane == n, blk_sum, gate)
                rank = jnp.zeros(gate.shape, F32)
                for m in range(n_past_blk):
                    gm = gate[:, m:m + 1]
                    beats = (gm > gate) | ((gm == gate) & (lane > m))
                    rank = rank + beats.astype(F32)
                sel = ((lane < n_past_blk) & (rank < MOBA_TOPK)).astype(F32)
                s_own = _dot_nt(q_exp_b, pad_rows(kn_ref[...]).astype(BF16))
                t_i = lax.broadcasted_iota(jnp.int32, s_own.shape, 0) // MOBA_HEADS
                s_own = jnp.where(lane <= t_i, s_own, MASK_VALUE)
                mx_acc = jnp.full((q_rows, MOBA_BLOCK), MASK_VALUE, F32)
                for n in range(n_past_blk):
                    cs = slice(n * MOBA_BLOCK, (n + 1) * MOBA_BLOCK)
                    blk = jnp.where(sel[:, n:n + 1] > 0.5, s_ref[:, cs], MASK_VALUE)
                    s_ref[:, cs] = blk
                    mx_acc = jnp.maximum(mx_acc, blk)
                mx = jnp.maximum(jnp.max(mx_acc, axis=1, keepdims=True),
                                 jnp.max(s_own, axis=1, keepdims=True))
                p_own = jnp.exp(s_own - mx)
                l_acc = jnp.zeros((q_rows, MOBA_BLOCK), F32)
                for n in range(n_past_blk):
                    cs = slice(n * MOBA_BLOCK, (n + 1) * MOBA_BLOCK)
                    blk = jnp.exp(s_ref[:, cs] - mx)
                    s_ref[:, cs] = blk
                    l_acc = l_acc + blk
                s_ref[:, past_len:past_len + LANES] = p_own
                inv_l = 1.0 / (jnp.sum(l_acc, axis=1, keepdims=True)
                               + jnp.sum(p_own, axis=1, keepdims=True))
        else:
            cc = c - n_chunks
            p_c = s_ref[:, cc * chunk_keys:(cc + 1) * chunk_keys].astype(BF16)
            acc = acc + _dot_nt(p_c, buf[slot].astype(BF16))
        yield c

    p_own = s_ref[:, past_len:past_len + LANES].astype(BF16)
    acc = (acc + _dot(p_own, pad_rows(vnew_ref[...]).astype(BF16))) * inv_l
    y = jnp.zeros((SAMPLE_ROWS, MOBA_DIM), F32)
    for t in range(n_q):
        grp = jnp.where(own_head, acc[t * MOBA_HEADS:(t + 1) * MOBA_HEADS, :], 0.0)
        y = jnp.where(sub_i == t, jnp.sum(grp, axis=0, keepdims=True), y)
    o_ref[...] = y


def _ffn_steps(x_ref, g_ref, w1_ref, w3_ref, w2_ref, o_ref, a_ref):
    x = x_ref[...]
    xb = _rms(x, g_ref[...]).astype(BF16)

    def hidden(s, w):
        h1 = _dot(xb, w1_ref[:, s:s + w])
        h3 = _dot(xb, w3_ref[:, s:s + w])
        a_ref[:, s:s + w] = (h1 * _sigmoid(h1) * h3).astype(BF16)

    def out_cols(c0, c1):
        o_ref[:, c0:c1] = x[:, c0:c1] + 0.5 * _dot(a_ref[...], w2_ref[:, c0:c1])

    half = D_MODEL // 2
    return ([functools.partial(hidden, s, w) for s, w in _FF_CHUNKS]
            + [functools.partial(out_cols, 0, half), functools.partial(out_cols, half, D_MODEL)])


def _ffn_moba_kernel(pt_ref, x_ref, g_ref, w1_ref, w3_ref, w2_ref, q_ref, kn_ref, vnew_ref,
                     ck_hbm, cv_hbm, o_ref, yb_ref, a_ref, buf, sem, s_ref, **moba_params):
    ffn = _ffn_steps(x_ref, g_ref, w1_ref, w3_ref, w2_ref, o_ref, a_ref)
    moba = _moba_sample_steps(pt_ref, q_ref, kn_ref, vnew_ref, ck_hbm, cv_hbm, yb_ref,
                              buf, sem, s_ref, **moba_params)
    n_chunk_steps = 2 * moba_params["n_pages"] // _PAGES_PER_CHUNK
    every = max(1, n_chunk_steps // len(ffn))
    for c in moba:
        if (c + 1) % every == 0 and ffn:
            ffn.pop(0)()
    for piece in ffn:
        piece()


def _ffn_moba(x, g, w1, w3, w2, page_table, q, k_new, v_new, cache_k, cache_v, *, layer, tm, n_q):
    m = x.shape[0]
    n_req, n_pages = page_table.shape
    assert m // tm == n_req
    depth, n_pool, page_size = cache_k.shape[:3]
    ck = jnp.transpose(cache_k, (0, 1, 3, 4, 2)).reshape(depth * n_pool, MOBA_DIM, page_size)
    cv = jnp.transpose(cache_v, (0, 1, 3, 4, 2)).reshape(depth * n_pool, MOBA_DIM, page_size)
    chunk_keys = _PAGES_PER_CHUNK * page_size
    past_len = n_pages * page_size
    q_rows = n_q * MOBA_HEADS
    assert (2 * n_pages // _PAGES_PER_CHUNK) % _RING_SLOTS == 0
    tile = pl.BlockSpec((tm, D_MODEL), lambda i, pt: (i, 0))
    rows = pl.BlockSpec((SAMPLE_ROWS, MOBA_DIM), lambda i, pt: (i, 0))
    hbm = pl.BlockSpec(memory_space=pl.ANY)
    scratch = [pltpu.VMEM((tm, D_FF), BF16),
               pltpu.VMEM((_RING_SLOTS, MOBA_DIM, chunk_keys), F32),
               pltpu.SemaphoreType.DMA((_RING_SLOTS,)),
               pltpu.VMEM((q_rows, past_len + LANES), F32)]
    blocks = (2 * 2 * _nbytes((tm, D_MODEL), F32) + 3 * _nbytes((D_MODEL, D_FF), BF16)
              + _nbytes((tm, D_FF), BF16)
              + _nbytes((_RING_SLOTS, MOBA_DIM, chunk_keys), F32)
              + _nbytes((q_rows, past_len + LANES), F32) + 8 * _nbytes((SAMPLE_ROWS, MOBA_DIM), F32))
    return pl.pallas_call(
        functools.partial(_ffn_moba_kernel, page_base=layer * n_pool, n_pages=n_pages,
                          page_size=page_size, n_q=n_q),
        grid_spec=pltpu.PrefetchScalarGridSpec(
            num_scalar_prefetch=1,
            grid=(n_req,),
            in_specs=[tile, _resident((1, D_MODEL)), _layer_resident(w1, layer),
                      _layer_resident(w3, layer), _layer_resident(w2, layer),
                      rows, rows, rows, hbm, hbm],
            out_specs=[tile, rows],
            scratch_shapes=scratch),
        out_shape=[jax.ShapeDtypeStruct((m, D_MODEL), F32),
                   jax.ShapeDtypeStruct((n_req * SAMPLE_ROWS, MOBA_DIM), F32)],
        compiler_params=pltpu.CompilerParams(
            dimension_semantics=("arbitrary",), vmem_limit_bytes=_vmem_limit(blocks)),
        name="ffn_moba",
    )(page_table.reshape(-1), x, g, w1, w3, w2, q, k_new, v_new, ck, cv)


def _sample_sgu_tables(sgu_w, sgu_b, n_b, t_new):
    t = jnp.arange(SAMPLE_ROWS)
    d = jnp.arange(t_new)
    src = t[None, :] - d[:, None]
    ok = (src >= 0) & (t[None, :] < t_new)
    w_td = sgu_w[:, t[None, :].clip(0, t_new - 1), src.clip(0, t_new - 1)]
    w_td = jnp.where(ok[None], w_td, 0.0)
    wd = jnp.repeat(jnp.transpose(w_td, (1, 2, 0)), SGU_GROUP_DIM, axis=2)
    bl = jnp.where((t < t_new)[:, None],
                   jnp.repeat(sgu_b[:, t.clip(0, t_new - 1)].T, SGU_GROUP_DIM, axis=1), 0.0)
    return jnp.tile(wd, (1, n_b, 1)), jnp.tile(bl, (n_b, 1))


def kernel(x_prompt, x_sample, mem_prompt, cache_k, cache_v, cache_mem_k, cache_mem_v, page_table,
           ffn1_norm, ffn1_w1, ffn1_w3, ffn1_w2, mix_norm, w_in, sgu_ln_g, sgu_ln_b, sgu_w, sgu_b,
           mem_norm, w_mem_kv, w_a_out, w_b_out, w_c_out, w_o, ffn2_norm, ffn2_w1, ffn2_w3, ffn2_w2,
           final_norm):
    depth = w_in.shape[0]
    bsz, t_len, _ = x_prompt.shape
    dec_b, t_new, _ = x_sample.shape
    n_pages = page_table.shape[1]
    page_size = cache_k.shape[2]
    past_len = n_pages * page_size
    mem_len = mem_prompt.shape[1]
    assert t_len % 512 == 0 and t_new <= SAMPLE_ROWS and past_len % MOBA_BLOCK == 0
    assert n_pages % _PAGES_PER_CHUNK == 0 and (_PAGES_PER_CHUNK * page_size) % MOBA_BLOCK == 0

    row = lambda a: a.reshape(1, -1)
    fin = row(final_norm)
    f1 = tuple(w.astype(BF16) for w in (ffn1_w1, ffn1_w3, ffn1_w2))
    f2 = tuple(w.astype(BF16) for w in (ffn2_w1, ffn2_w3, ffn2_w2))
    win, wkv = w_in.astype(BF16), w_mem_kv.astype(BF16)
    outs_w = tuple(w.astype(BF16) for w in (w_a_out, w_b_out, w_c_out, w_o))

    tm_p = 512
    cos_p, slo_p, shi_p = _rope_tables(jnp.arange(t_len, dtype=jnp.int32))
    hp = x_prompt.reshape(bsz * t_len, D_MODEL)
    mem = mem_prompt.reshape(bsz * mem_len, D_MODEL)
    kv_t, mkp, mvp = (), [], []

    m_s = dec_b * SAMPLE_ROWS
    pos_s = past_len + jnp.arange(SAMPLE_ROWS, dtype=jnp.int32)
    cos_s, slo_s, shi_s = (jnp.tile(t, (dec_b, 1)) for t in _rope_tables(pos_s))
    hs = jnp.pad(x_sample, ((0, 0), (0, SAMPLE_ROWS - t_new), (0, 0))).reshape(m_s, D_MODEL)
    n_b = 8
    mem_k = cache_mem_k.reshape(depth, dec_b, mem_len * MEM_HEADS, MEM_HEAD_DIM)
    mem_v = cache_mem_v.reshape(depth, dec_b, mem_len * MEM_HEADS, MEM_HEAD_DIM)
    ks_, vs_, sv_ = [], [], []
    unpad = lambda a: a.reshape(dec_b, SAMPLE_ROWS, -1)[:, :t_new]

    for l in range(depth):
        wd, bl = _sample_sgu_tables(sgu_w[l], sgu_b[l], n_b, t_new)
        hs = _ffn(hs, row(ffn1_norm[l]), *f1, fin, layer=l, tm=m_s, final_norm=False)
        u_s, vn_s, q_s, k_s, v_s, cq_s, gate_s = _inproj(
            hs, row(mix_norm[l]), win, row(sgu_ln_g[l]), row(sgu_ln_b[l]), cos_s, slo_s, shi_s,
            layer=l, tm=m_s, t_major_kv=False, prev_kv_t=(), act_dtype=F32)
        ya_s, yc_s = _premix_sample(u_s, vn_s, cq_s, mem_k, mem_v, wd, bl, layer=l, n_b=n_b)

        mk, mv = _memkv(mem, row(mem_norm[l]), wkv, layer=l, tm=512)
        hp, yb_s = _ffn_moba(hp, row(ffn1_norm[l]), *f1, page_table, q_s, k_s, v_s, cache_k, cache_v,
                             layer=l, tm=tm_p, n_q=t_new)
        u, vn, q, k, v, cq, gate, *kv_t = _inproj(
            hp, row(mix_norm[l]), win, row(sgu_ln_g[l]), row(sgu_ln_b[l]), cos_p, slo_p, shi_p,
            layer=l, tm=tm_p, t_major_kv=True, prev_kv_t=kv_t, act_dtype=BF16)
        yb = _moba_prompt(q.reshape(bsz, t_len, MOBA_DIM), k.reshape(bsz, t_len, MOBA_DIM),
                          v.reshape(bsz, t_len, MOBA_DIM), out_dtype=BF16).reshape(bsz * t_len, MOBA_DIM)
        hp = _mix_prompt(u, vn, cq, gate, yb, hp, mk.reshape(bsz, mem_len, MEM_DIM),
                         mv.reshape(bsz, mem_len, MEM_DIM), sgu_w, sgu_b[l].T, *outs_w,
                         layer=l, tm=tm_p, t_len=t_len)
        hp = _ffn(hp, row(ffn2_norm[l]), *f2, fin, layer=l, tm=2 * tm_p, final_norm=(l == depth - 1))
        mkp.append(mk.reshape(bsz, mem_len, MEM_HEADS, MEM_HEAD_DIM))
        mvp.append(mv.reshape(bsz, mem_len, MEM_HEADS, MEM_HEAD_DIM))

        hs = _merge_sample(hs, ya_s, yb_s, yc_s, gate_s, *outs_w, layer=l)
        hs = _ffn(hs, row(ffn2_norm[l]), *f2, fin, layer=l, tm=m_s, final_norm=(l == depth - 1))
        ks_.append(unpad(k_s).reshape(dec_b, t_new, MOBA_HEADS, MOBA_HEAD_DIM))
        vs_.append(unpad(v_s).reshape(dec_b, t_new, MOBA_HEADS, MOBA_HEAD_DIM))
        sv_.append(unpad(vn_s))
    y_prompt = hp.reshape(bsz, t_len, D_MODEL)
    y_sample = unpad(hs)

    def token_minor_to_rows(stack):
        return jnp.transpose(stack.reshape(depth, bsz, MOBA_HEADS, MOBA_HEAD_DIM, t_len), (0, 1, 4, 2, 3))

    return (y_prompt, y_sample, token_minor_to_rows(kv_t[0]), token_minor_to_rows(kv_t[1]),
            jnp.stack(mkp), jnp.stack(mvp), jnp.stack(ks_), jnp.stack(vs_), jnp.stack(sv_))
```

```python
import functools

import jax
import jax.numpy as jnp
import numpy as np
from jax import lax
from jax.experimental import pallas as pl
from jax.experimental.pallas import tpu as pltpu

F32 = jnp.float32
BF16 = jnp.bfloat16

D_MODEL = 1024
SGU_GROUPS = 4
SGU_GROUP_DIM = 128
SGU_DIM = 512
CHUNK = 128
MOBA_HEADS = 8
MOBA_HEAD_DIM = 64
MOBA_DIM = 512
MOBA_BLOCK = 256
MOBA_TOPK = 3
MEM_HEADS = 4
MEM_HEAD_DIM = 128
MEM_DIM = 512
D_FF = 2816
ROPE_THETA = 10000.0
NORM_EPS = 1e-6
MASK_VALUE = -1e30
LOG2_E = float(np.log2(np.e))
IN_COLS = 6144
GATE_COL0 = 3072

V7X_VMEM_BYTES = 64 * 1024 * 1024
V7X_VMEM_RESERVE = 4 * 1024 * 1024
LANES = 128
SUBLANES = 8
SAMPLE_ROWS = SUBLANES
TEMP_ALLOWANCE = 12 * 1024 * 1024

NT_DIMS = (((1,), (1,)), ((), ()))


def _vmem_limit(block_bytes):
    return int(min(block_bytes + TEMP_ALLOWANCE, V7X_VMEM_BYTES - V7X_VMEM_RESERVE))


def _nbytes(shape, dtype):
    return int(np.prod(shape)) * jnp.dtype(dtype).itemsize


def _resident(shape):
    nd = len(shape)
    return pl.BlockSpec(shape, lambda *_: (0,) * nd, pipeline_mode=pl.Buffered(1))


def _layer_resident(arr, layer):
    nd = arr.ndim
    return pl.BlockSpec((None,) + arr.shape[1:], lambda *_: (layer,) + (0,) * (nd - 1),
                        pipeline_mode=pl.Buffered(1))


def _dot(a, b):
    return jnp.dot(a, b, preferred_element_type=F32)


def _dot_nt(a, b, precision=None):
    return lax.dot_general(a, b, NT_DIMS, precision=precision, preferred_element_type=F32)


def _rms(x, g):
    return x * lax.rsqrt(jnp.mean(x * x, axis=-1, keepdims=True) + NORM_EPS) * g


def _sigmoid(x):
    return 1.0 / (1.0 + jnp.exp(-x))


def _gelu(x):
    return 0.5 * x * (1.0 + lax.erf(x * np.float32(np.sqrt(0.5))))


def _softmax_rows(s):
    p = jnp.exp(s - jnp.max(s, axis=-1, keepdims=True))
    return p, 1.0 / jnp.sum(p, axis=-1, keepdims=True)


_FF_CHUNK = 512
_FF_CHUNKS = tuple((s, min(_FF_CHUNK, D_FF - s)) for s in range(0, D_FF, _FF_CHUNK))


def _ffn_kernel(x_ref, g_ref, w1_ref, w3_ref, w2_ref, fg_ref, o_ref, a_ref, *, final_norm):
    x = x_ref[...]
    xb = _rms(x, g_ref[...]).astype(BF16)
    for s, w in _FF_CHUNKS:
        h1 = _dot(xb, w1_ref[:, s:s + w])
        h3 = _dot(xb, w3_ref[:, s:s + w])
        a_ref[:, s:s + w] = (h1 * _sigmoid(h1) * h3).astype(BF16)
    y = x + 0.5 * _dot(a_ref[...], w2_ref[...])
    if final_norm:
        y = _rms(y, fg_ref[...])
    o_ref[...] = y


def _ffn(x, g, w1, w3, w2, fg, *, layer, tm, final_norm):
    m = x.shape[0]
    blocks = (2 * 2 * _nbytes((tm, D_MODEL), F32) + 3 * _nbytes((D_MODEL, D_FF), BF16)
              + _nbytes((tm, D_FF), BF16))
    return pl.pallas_call(
        functools.partial(_ffn_kernel, final_norm=final_norm),
        grid=(m // tm,),
        in_specs=[
            pl.BlockSpec((tm, D_MODEL), lambda i: (i, 0)),
            _resident((1, D_MODEL)),
            _layer_resident(w1, layer),
            _layer_resident(w3, layer),
            _layer_resident(w2, layer),
            _resident((1, D_MODEL)),
        ],
        out_specs=pl.BlockSpec((tm, D_MODEL), lambda i: (i, 0)),
        out_shape=jax.ShapeDtypeStruct((m, D_MODEL), F32),
        scratch_shapes=[pltpu.VMEM((tm, D_FF), BF16)],
        compiler_params=pltpu.CompilerParams(
            dimension_semantics=("parallel",), vmem_limit_bytes=_vmem_limit(blocks)),
        name="ffn",
    )(x, g, w1, w3, w2, fg)


def _rope(h, cos, sin_lo, sin_hi):
    outs = []
    for c in range(h.shape[1] // LANES):
        s = h[:, c * LANES:(c + 1) * LANES]
        outs.append(s * cos + pltpu.roll(s, LANES - 32, 1) * sin_lo + pltpu.roll(s, 32, 1) * sin_hi)
    return jnp.concatenate(outs, axis=1)


def _inproj_kernel(x_ref, g_ref, w_ref, lng_ref, lnb_ref, cos_ref, slo_ref, shi_ref, *refs, n_prev):
    prev_refs, (u_ref, vn_ref, q_ref, k_ref, v_ref, cq_ref, gate_ref), t_refs = (
        refs[:n_prev], refs[n_prev:n_prev + 7], refs[n_prev + 7:])
    xb = _rms(x_ref[...], g_ref[...]).astype(BF16)

    def proj(c0, w=512):
        return _dot(xb, w_ref[:, c0:c0 + w])

    u_ref[...] = _gelu(proj(0)).astype(u_ref.dtype)
    a = _gelu(proj(512))
    ac = a - jnp.mean(a, axis=-1, keepdims=True)
    vn_ref[...] = (ac * lax.rsqrt(jnp.mean(ac * ac, axis=-1, keepdims=True) + NORM_EPS)
                   * lng_ref[...] + lnb_ref[...]).astype(vn_ref.dtype)
    cos, slo, shi = cos_ref[...], slo_ref[...], shi_ref[...]
    q_ref[...] = _rope(proj(1024), cos, slo, shi)
    k = _rope(proj(1536), cos, slo, shi)
    v = proj(2048)
    k_ref[...] = k
    v_ref[...] = v
    for t_ref, new, prev_ref in zip(t_refs, (k, v), prev_refs or (None, None)):
        n_before = t_ref.shape[0] - 1
        if n_before:
            t_ref[:n_before] = prev_ref[...]
        t_ref[n_before] = new.T
    cq_ref[...] = proj(2560).astype(cq_ref.dtype)
    for c in range(6):
        gate_ref[:, c * 512:(c + 1) * 512] = _sigmoid(proj(GATE_COL0 + c * 512)).astype(gate_ref.dtype)


def _inproj(x, g, w, lng, lnb, cos, slo, shi, *, layer, tm, t_major_kv, prev_kv_t, act_dtype):
    m = x.shape[0]
    t_len = cos.shape[0]
    n_tab = t_len // tm
    out_cols = (512,) * 6 + (3 * D_MODEL,)
    blocks = (2 * _nbytes((tm, D_MODEL), F32) + _nbytes((D_MODEL, IN_COLS), BF16)
              + 2 * _nbytes((tm, IN_COLS), F32) + 6 * _nbytes((tm, LANES), F32))
    tab = pl.BlockSpec((tm, LANES), lambda i: (i % n_tab, 0))
    out_specs = [pl.BlockSpec((tm, c), lambda i: (i, 0)) for c in out_cols]
    out_dtypes = (act_dtype, act_dtype, F32, F32, F32, act_dtype, act_dtype)
    out_shape = [jax.ShapeDtypeStruct((m, c), dt) for c, dt in zip(out_cols, out_dtypes)]
    prev_specs = []
    if t_major_kv:
        def stack_spec(n):
            return pl.BlockSpec((n, None, MOBA_DIM, tm), lambda i: (0, i // n_tab, 0, i % n_tab))

        assert len(prev_kv_t) == (2 if layer else 0)
        blocks += 4 * (2 * layer + 1) * _nbytes((MOBA_DIM, tm), F32)
        prev_specs = [stack_spec(layer)] * len(prev_kv_t)
        out_specs += [stack_spec(layer + 1)] * 2
        out_shape += [jax.ShapeDtypeStruct((layer + 1, m // t_len, MOBA_DIM, t_len), F32)] * 2
    return pl.pallas_call(
        functools.partial(_inproj_kernel, n_prev=len(prev_specs)),
        grid=(m // tm,),
        in_specs=[
            pl.BlockSpec((tm, D_MODEL), lambda i: (i, 0)),
            _resident((1, D_MODEL)),
            _layer_resident(w, layer),
            _resident((1, SGU_DIM)),
            _resident((1, SGU_DIM)),
            tab, tab, tab,
        ] + prev_specs,
        out_specs=out_specs,
        out_shape=out_shape,
        compiler_params=pltpu.CompilerParams(
            dimension_semantics=("parallel",), vmem_limit_bytes=_vmem_limit(blocks)),
        name="inproj",
    )(x, g, w, lng, lnb, cos, slo, shi, *prev_kv_t)


def _rope_tables(pos):
    half = MOBA_HEAD_DIM // 2
    inv = ROPE_THETA ** (-jnp.arange(half, dtype=F32) * 2.0 / MOBA_HEAD_DIM)
    ang = pos.astype(F32)[:, None] * inv[None, :]
    cos, sin = jnp.tile(jnp.cos(ang), (1, 4)), jnp.tile(jnp.sin(ang), (1, 4))
    first_half = (jnp.arange(LANES) % MOBA_HEAD_DIM) < half
    return cos, jnp.where(first_half, -sin, 0.0), jnp.where(first_half, 0.0, sin)


def _memkv_kernel(m_ref, g_ref, w_ref, k_ref, v_ref, kc_ref, vc_ref):
    xb = _rms(m_ref[...], g_ref[...]).astype(BF16)
    k = _dot(xb, w_ref[:, :MEM_DIM])
    v = _dot(xb, w_ref[:, MEM_DIM:])
    k_ref[...] = k
    v_ref[...] = v
    rows = k.shape[0]
    for h in range(MEM_HEADS):
        sl = slice(h * MEM_HEAD_DIM, (h + 1) * MEM_HEAD_DIM)
        kc_ref[pl.ds(h, rows, stride=MEM_HEADS), :] = k[:, sl]
        vc_ref[pl.ds(h, rows, stride=MEM_HEADS), :] = v[:, sl]


def _memkv(mem, g, w, *, layer, tm):
    m = mem.shape[0]
    blocks = (2 * _nbytes((tm, D_MODEL), F32) + _nbytes((D_MODEL, 2 * MEM_DIM), BF16)
              + 8 * _nbytes((tm, MEM_DIM), F32))
    cache_rows = pl.BlockSpec((tm * MEM_HEADS, MEM_HEAD_DIM), lambda i: (i, 0))
    return pl.pallas_call(
        _memkv_kernel,
        grid=(m // tm,),
        in_specs=[pl.BlockSpec((tm, D_MODEL), lambda i: (i, 0)), _resident((1, D_MODEL)),
                  _layer_resident(w, layer)],
        out_specs=[pl.BlockSpec((tm, MEM_DIM), lambda i: (i, 0))] * 2 + [cache_rows] * 2,
        out_shape=([jax.ShapeDtypeStruct((m, MEM_DIM), F32)] * 2
                   + [jax.ShapeDtypeStruct((m * MEM_HEADS, MEM_HEAD_DIM), F32)] * 2),
        compiler_params=pltpu.CompilerParams(
            dimension_semantics=("parallel",), vmem_limit_bytes=_vmem_limit(blocks)),
        name="memkv",
    )(mem, g, w)


def _topk_select(g, n_past):
    row = lax.broadcasted_iota(jnp.int32, g.shape, 0)
    rank = jnp.zeros(g.shape, F32)
    for m in range(n_past):
        gm = g[m:m + 1, :]
        beats = (gm > g) | ((gm == g) & (row > m))
        rank = rank + beats.astype(F32)
    return (row < n_past) & (rank < MOBA_TOPK)


_MOBA_SLABS = 2


def _moba_prompt_kernel(q_ref, k_ref, v_ref, o_ref):
    t_len = q_ref.shape[1]
    n_blk = t_len // MOBA_BLOCK
    n_slabs = q_ref.shape[2] // LANES
    lane = lax.broadcasted_iota(jnp.int32, (1, LANES), 1)
    head_masks = (lane < MOBA_HEAD_DIM, lane >= MOBA_HEAD_DIM)
    r_i = lax.broadcasted_iota(jnp.int32, (2 * MOBA_BLOCK, MOBA_BLOCK), 0) & (MOBA_BLOCK - 1)
    c_i = lax.broadcasted_iota(jnp.int32, (2 * MOBA_BLOCK, MOBA_BLOCK), 1)
    causal2 = c_i <= r_i
    scale = MOBA_HEAD_DIM ** -0.5 * LOG2_E
    ones = jnp.ones((t_len, LANES), BF16)

    kb, v_aug, k_mean = [], [], []
    for sl in range(n_slabs):
        k = k_ref[0, :, sl * LANES:(sl + 1) * LANES]
        kb.append(k.astype(BF16))
        v_aug.append(jnp.concatenate(
            [v_ref[0, :, sl * LANES:(sl + 1) * LANES].astype(BF16), ones], axis=1))
        k_mean.append(jnp.sum(k.reshape(n_blk, MOBA_BLOCK, LANES), axis=1) * (1.0 / MOBA_BLOCK))

    def scores(sl, qi):
        q_blk = q_ref[0, qi * MOBA_BLOCK:(qi + 1) * MOBA_BLOCK, sl * LANES:(sl + 1) * LANES]
        q2 = jnp.concatenate([jnp.where(hm, q_blk, 0.0) for hm in head_masks], axis=0)
        return q2, _dot_nt((q2 * scale).astype(BF16), kb[sl][:(qi + 1) * MOBA_BLOCK])

    def probabilities(sl, qi, q2, s):
        parts = []
        if qi > MOBA_TOPK:
            gate = _dot_nt(k_mean[sl], q2, precision=lax.Precision.HIGHEST)
            sel = _topk_select(gate, qi).astype(F32)
            sel_t = jnp.concatenate(
                [sel, jnp.zeros((LANES - n_blk, 2 * MOBA_BLOCK), F32)], axis=0).T
            for n in range(qi):
                keep = sel_t[:, n:n + 1] > 0.5
                parts.append(jnp.where(keep, s[:, n * MOBA_BLOCK:(n + 1) * MOBA_BLOCK], MASK_VALUE))
        else:
            for n in range(qi):
                parts.append(s[:, n * MOBA_BLOCK:(n + 1) * MOBA_BLOCK])
        parts.append(jnp.where(causal2, s[:, qi * MOBA_BLOCK:], MASK_VALUE))
        sm = jnp.concatenate(parts, axis=1) if len(parts) > 1 else parts[0]
        return jnp.exp2(sm - jnp.max(sm, axis=-1, keepdims=True)).astype(BF16)

    def weighted_values(sl, qi, p):
        o2 = _dot(p, v_aug[sl][:(qi + 1) * MOBA_BLOCK])
        o2 = o2[:, :LANES] * (1.0 / o2[:, LANES:])
        o_ref[0, qi * MOBA_BLOCK:(qi + 1) * MOBA_BLOCK, sl * LANES:(sl + 1) * LANES] = jnp.where(
            head_masks[0], o2[:MOBA_BLOCK], o2[MOBA_BLOCK:]).astype(o_ref.dtype)

    slabs = range(n_slabs)
    ahead = [scores(sl, 0) for sl in slabs]
    pending = None
    for qi in range(n_blk):
        cur = ahead
        if qi + 1 < n_blk:
            ahead = [scores(sl, qi + 1) for sl in slabs]
        probs = [probabilities(sl, qi, *cur[sl]) for sl in slabs]
        if pending is not None:
            for sl in slabs:
                weighted_values(sl, qi - 1, pending[sl])
        pending = probs
    for sl in slabs:
        weighted_values(sl, n_blk - 1, pending[sl])


def _moba_prompt(q, k, v, *, out_dtype):
    bsz, t_len, _ = q.shape
    width = _MOBA_SLABS * LANES
    spec = pl.BlockSpec((1, t_len, width), lambda b, h: (b, 0, h))
    blocks = _MOBA_SLABS * (8 * _nbytes((t_len, LANES), F32) + 3 * _nbytes((t_len, LANES), BF16)
                            + 8 * _nbytes((MOBA_BLOCK, t_len), F32))
    return pl.pallas_call(
        _moba_prompt_kernel,
        grid=(bsz, MOBA_DIM // width),
        in_specs=[spec, spec, spec],
        out_specs=spec,
        out_shape=jax.ShapeDtypeStruct((bsz, t_len, MOBA_DIM), out_dtype),
        compiler_params=pltpu.CompilerParams(
            dimension_semantics=("parallel", "parallel"), vmem_limit_bytes=_vmem_limit(blocks)),
        name="moba_prompt",
    )(q, k, v)


def _merge(x, ya, yb, yc, gate_ref, wa_ref, wb_ref, wc_ref, wo_ref):
    t = (gate_ref[:, 0:D_MODEL] * _dot(ya.astype(BF16), wa_ref[...])
         + gate_ref[:, D_MODEL:2 * D_MODEL] * _dot(yb.astype(BF16), wb_ref[...])
         + gate_ref[:, 2 * D_MODEL:3 * D_MODEL] * _dot(yc.astype(BF16), wc_ref[...]))
    return x + _dot(t.astype(BF16), wo_ref[...])


def _mem_attention(cq, mkb, mvb):
    scale = MEM_HEAD_DIM ** -0.5
    outs = []
    for h in range(MEM_HEADS):
        sl = slice(h * MEM_HEAD_DIM, (h + 1) * MEM_HEAD_DIM)
        s = _dot_nt(cq[:, sl].astype(BF16), mkb[:, sl]) * scale
        p, inv_l = _softmax_rows(s)
        outs.append(_dot(p.astype(BF16), mvb[:, sl]) * inv_l)
    return jnp.concatenate(outs, axis=1)


def _mix_prompt_kernel(u_ref, vn_ref, cq_ref, gate_ref, yb_ref, x_ref, mk_ref, mv_ref,
                       sw_ref, sb_ref, wa_ref, wb_ref, wc_ref, wo_ref, o_ref):
    tm = u_ref.shape[0]
    r_i = lax.broadcasted_iota(jnp.int32, (CHUNK, CHUNK), 0)
    c_i = lax.broadcasted_iota(jnp.int32, (CHUNK, CHUNK), 1)
    wm = [jnp.where(c_i <= r_i, sw_ref[g], 0.0).astype(BF16) for g in range(SGU_GROUPS)]
    rows = []
    for ch in range(tm // CHUNK):
        rs = slice(ch * CHUNK, (ch + 1) * CHUNK)
        cols = []
        for g in range(SGU_GROUPS):
            cs = slice(g * SGU_GROUP_DIM, (g + 1) * SGU_GROUP_DIM)
            y = _dot(wm[g], vn_ref[rs, cs].astype(BF16)) + sb_ref[:, g:g + 1]
            cols.append(u_ref[rs, cs].astype(F32) * y)
        rows.append(jnp.concatenate(cols, axis=1))
    ya = jnp.concatenate(rows, axis=0)
    yc = _mem_attention(cq_ref[...], mk_ref[0].astype(BF16), mv_ref[0].astype(BF16))
    o_ref[...] = _merge(x_ref[...], ya, yb_ref[...], yc, gate_ref, wa_ref, wb_ref, wc_ref, wo_ref)


def _mix_prompt(u, vn, cq, gate, yb, x, mk, mv, sw, sb_t, wa, wb, wc, wo, *, layer, tm, t_len):
    m = x.shape[0]
    per_b = t_len // tm

    def rows(c):
        return pl.BlockSpec((tm, c), lambda i: (i, 0))

    mem_spec = pl.BlockSpec((1,) + mk.shape[1:], lambda i: (i // per_b, 0, 0))
    blocks = (2 * _nbytes((tm, 4 * 512 + 3 * D_MODEL + 2 * D_MODEL), F32)
              + 4 * _nbytes(mk.shape[1:], F32) + _nbytes((3 * 512 + D_MODEL, D_MODEL), BF16))
    return pl.pallas_call(
        _mix_prompt_kernel,
        grid=(m // tm,),
        in_specs=[rows(512), rows(512), rows(512), rows(3 * D_MODEL), rows(512), rows(D_MODEL),
                  mem_spec, mem_spec,
                  _layer_resident(sw, layer), _resident(sb_t.shape),
                  _layer_resident(wa, layer), _layer_resident(wb, layer),
                  _layer_resident(wc, layer), _layer_resident(wo, layer)],
        out_specs=rows(D_MODEL),
        out_shape=jax.ShapeDtypeStruct((m, D_MODEL), F32),
        compiler_params=pltpu.CompilerParams(
            dimension_semantics=("parallel",), vmem_limit_bytes=_vmem_limit(blocks)),
        name="mix_prompt",
    )(u, vn, cq, gate, yb, x, mk, mv, sw, sb_t, wa, wb, wc, wo)


def _premix_sample_kernel(u_ref, vn_ref, cq_ref, mk_ref, mv_ref, wd_ref, bl_ref, ya_ref, yc_ref):
    vn = vn_ref[...]
    y = bl_ref[...] + wd_ref[0] * vn
    for d in range(1, wd_ref.shape[0]):
        y = y + wd_ref[d] * pltpu.roll(vn, d, 0)
    ya_ref[...] = u_ref[...] * y

    q_rows = MEM_HEADS * SAMPLE_ROWS
    n_keys = mk_ref.shape[1]
    row_head = lax.broadcasted_iota(jnp.int32, (q_rows, n_keys), 0) // SAMPLE_ROWS
    key_head = lax.broadcasted_iota(jnp.int32, (q_rows, n_keys), 1) % MEM_HEADS
    same_head = row_head == key_head
    scale = MEM_HEAD_DIM ** -0.5
    for b in range(mk_ref.shape[0]):
        rs = slice(b * SAMPLE_ROWS, (b + 1) * SAMPLE_ROWS)
        q8 = cq_ref[rs, :]
        q_h = jnp.concatenate([q8[:, h * MEM_HEAD_DIM:(h + 1) * MEM_HEAD_DIM]
                               for h in range(MEM_HEADS)], axis=0)
        s = _dot_nt(q_h.astype(BF16), mk_ref[b].astype(BF16)) * scale
        p, inv_l = _softmax_rows(jnp.where(same_head, s, MASK_VALUE))
        o = _dot(p.astype(BF16), mv_ref[b].astype(BF16)) * inv_l
        yc_ref[rs, :] = jnp.concatenate([o[h * SAMPLE_ROWS:(h + 1) * SAMPLE_ROWS, :]
                                         for h in range(MEM_HEADS)], axis=1)


def _premix_sample(u, vn, cq, mk, mv, wd, bl, *, layer, n_b):
    m = u.shape[0]
    tm = n_b * SAMPLE_ROWS
    rows = pl.BlockSpec((tm, 512), lambda i: (i, 0))
    mem_block = (n_b,) + mk.shape[2:]
    mem_spec = pl.BlockSpec((None,) + mem_block, lambda i: (layer, i, 0, 0))
    blocks = (2 * 5 * _nbytes((tm, 512), F32) + 4 * _nbytes(mem_block, F32)
              + _nbytes(wd.shape, F32) + _nbytes(bl.shape, F32))
    return pl.pallas_call(
        _premix_sample_kernel,
        grid=(m // tm,),
        in_specs=[rows, rows, rows, mem_spec, mem_spec, _resident(wd.shape), _resident(bl.shape)],
        out_specs=[rows, rows],
        out_shape=[jax.ShapeDtypeStruct((m, 512), F32)] * 2,
        compiler_params=pltpu.CompilerParams(
            dimension_semantics=("parallel",), vmem_limit_bytes=_vmem_limit(blocks)),
        name="premix_sample",
    )(u, vn, cq, mk, mv, wd, bl)


def _merge_kernel(x_ref, ya_ref, yb_ref, yc_ref, gate_ref, wa_ref, wb_ref, wc_ref, wo_ref, o_ref):
    o_ref[...] = _merge(x_ref[...], ya_ref[...], yb_ref[...], yc_ref[...], gate_ref,
                        wa_ref, wb_ref, wc_ref, wo_ref)


def _merge_sample(x, ya, yb, yc, gate, wa, wb, wc, wo, *, layer):
    m = x.shape[0]
    acts, weights = (x, ya, yb, yc, gate), (wa, wb, wc, wo)
    args = acts + weights
    blocks = (sum(_nbytes(a.shape, a.dtype) for a in acts) + _nbytes(x.shape, F32)
              + sum(_nbytes(w.shape[1:], w.dtype) for w in weights))
    return pl.pallas_call(
        _merge_kernel,
        grid=(1,),
        in_specs=[_resident(a.shape) for a in acts] + [_layer_resident(w, layer) for w in weights],
        out_specs=pl.BlockSpec((m, D_MODEL), lambda i: (0, 0)),
        out_shape=jax.ShapeDtypeStruct((m, D_MODEL), F32),
        compiler_params=pltpu.CompilerParams(
            dimension_semantics=("arbitrary",), vmem_limit_bytes=_vmem_limit(blocks)),
        name="merge_sample",
    )(*args)


_PAGES_PER_CHUNK = 16
_RING_SLOTS = 4
_RING_AHEAD = _RING_SLOTS - 1


def _moba_sample_steps(pt_ref, q_ref, kn_ref, vnew_ref, ck_hbm, cv_hbm, o_ref,
                       buf, sem, s_ref, *, page_base, n_pages, page_size, n_q):
    b = pl.program_id(0)
    n_req = pl.num_programs(0)
    chunk_keys = _PAGES_PER_CHUNK * page_size
    n_chunks = n_pages // _PAGES_PER_CHUNK
    blocks_per_chunk = chunk_keys // MOBA_BLOCK
    n_past_blk = n_chunks * blocks_per_chunk
    past_len = n_pages * page_size
    q_rows = n_q * MOBA_HEADS

    def copies(req, c, slot):
        src = ck_hbm if c < n_chunks else cv_hbm
        first = req * n_pages + (c % n_chunks) * _PAGES_PER_CHUNK
        return [pltpu.make_async_copy(src.at[pt_ref[first + p] + page_base],
                                      buf.at[slot, :, pl.ds(p * page_size, page_size)],
                                      sem.at[slot])
                for p in range(_PAGES_PER_CHUNK)]

    def start(req, c, slot):
        for cp in copies(req, c, slot):
            cp.start()

    def wait(req, c, slot):
        for cp in copies(req, c, slot):
            cp.wait()

    n_total = 2 * n_chunks

    @pl.when(b == 0)
    def _():
        for c0 in range(_RING_AHEAD):
            start(b, c0, c0 % _RING_SLOTS)

    sub_i = lax.broadcasted_iota(jnp.int32, (SAMPLE_ROWS, MOBA_DIM), 0)
    own_head = sub_i == lax.broadcasted_iota(jnp.int32, (SAMPLE_ROWS, MOBA_DIM), 1) // MOBA_HEAD_DIM
    q8 = q_ref[...] * (MOBA_HEAD_DIM ** -0.5)
    q_exp = jnp.concatenate(
        [jnp.where(own_head, jnp.broadcast_to(q8[t:t + 1, :], (MOBA_HEADS, MOBA_DIM)), 0.0)
         for t in range(n_q)], axis=0)
    q_exp_b = q_exp.astype(BF16)

    def pad_rows(x8):
        return jnp.concatenate([x8, jnp.zeros((LANES - SAMPLE_ROWS, x8.shape[1]), F32)], axis=0)

    acc = jnp.zeros((q_rows, MOBA_DIM), F32)
    inv_l = None
    for c in range(n_total):
        slot = c % _RING_SLOTS
        wait(b, c, slot)
        nxt = c + _RING_AHEAD
        if nxt < n_total:
            start(b, nxt, nxt % _RING_SLOTS)
        else:
            @pl.when(b + 1 < n_req)
            def _(nxt=nxt):
                start(b + 1, nxt - n_total, nxt % _RING_SLOTS)

        if c < n_chunks:
            s_ref[:, c * chunk_keys:(c + 1) * chunk_keys] = _dot(q_exp_b, buf[slot].astype(BF16))
            if c == n_chunks - 1:
                lane = lax.broadcasted_iota(jnp.int32, (q_rows, LANES), 1)
                gate = jnp.zeros((q_rows, LANES), F32)
                for n in range(n_past_blk):
                    blk_sum = jnp.sum(s_ref[:, n * MOBA_BLOCK:(n + 1) * MOBA_BLOCK], axis=1, keepdims=True)
                    gate = jnp.where(lane == n, blk_sum, gate)
                rank = jnp.zeros(gate.shape, F32)
                for m in range(n_past_blk):
                    gm = gate[:, m:m + 1]
                    beats = (gm > gate) | ((gm == gate) & (lane > m))
                    rank = rank + beats.astype(F32)
                sel = ((lane < n_past_blk) & (rank < MOBA_TOPK)).astype(F32)
                s_own = _dot_nt(q_exp_b, pad_rows(kn_ref[...]).astype(BF16))
                t_i = lax.broadcasted_iota(jnp.int32, s_own.shape, 0) // MOBA_HEADS
                s_own = jnp.where(lane <= t_i, s_own, MASK_VALUE)
                mx_acc = jnp.full((q_rows, MOBA_BLOCK), MASK_VALUE, F32)
                for n in range(n_past_blk):
                    cs = slice(n * MOBA_BLOCK, (n + 1) * MOBA_BLOCK)
                    blk = jnp.where(sel[:, n:n + 1] > 0.5, s_ref[:, cs], MASK_VALUE)
                    s_ref[:, cs] = blk
                    mx_acc = jnp.maximum(mx_acc, blk)
                mx = jnp.maximum(jnp.max(mx_acc, axis=1, keepdims=True),
                                 jnp.max(s_own, axis=1, keepdims=True))
                p_own = jnp.exp(s_own - mx)
                l_acc = jnp.zeros((q_rows, MOBA_BLOCK), F32)
                for n in range(n_past_blk):
                    cs = slice(n * MOBA_BLOCK, (n + 1) * MOBA_BLOCK)
                    blk = jnp.exp(s_ref[:, cs] - mx)
                    s_ref[:, cs] = blk
                    l_acc = l_acc + blk
                s_ref[:, past_len:past_len + LANES] = p_own
                inv_l = 1.0 / (jnp.sum(l_acc, axis=1, keepdims=True)
                               + jnp.sum(p_own, axis=1, keepdims=True))
        else:
            cc = c - n_chunks
            p_c = s_ref[:, cc * chunk_keys:(cc + 1) * chunk_keys].astype(BF16)
            acc = acc + _dot_nt(p_c, buf[slot].astype(BF16))
        yield c

    p_own = s_ref[:, past_len:past_len + LANES].astype(BF16)
    acc = (acc + _dot(p_own, pad_rows(vnew_ref[...]).astype(BF16))) * inv_l
    y = jnp.zeros((SAMPLE_ROWS, MOBA_DIM), F32)
    for t in range(n_q):
        grp = jnp.where(own_head, acc[t * MOBA_HEADS:(t + 1) * MOBA_HEADS, :], 0.0)
        y = jnp.where(sub_i == t, jnp.sum(grp, axis=0, keepdims=True), y)
    o_ref[...] = y


def _ffn_steps(x_ref, g_ref, w1_ref, w3_ref, w2_ref, o_ref, a_ref):
    x = x_ref[...]
    xb = _rms(x, g_ref[...]).astype(BF16)

    def hidden(s, w):
        h1 = _dot(xb, w1_ref[:, s:s + w])
        h3 = _dot(xb, w3_ref[:, s:s + w])
        a_ref[:, s:s + w] = (h1 * _sigmoid(h1) * h3).astype(BF16)

    def out_cols(c0, c1):
        o_ref[:, c0:c1] = x[:, c0:c1] + 0.5 * _dot(a_ref[...], w2_ref[:, c0:c1])

    half = D_MODEL // 2
    return ([functools.partial(hidden, s, w) for s, w in _FF_CHUNKS]
            + [functools.partial(out_cols, 0, half), functools.partial(out_cols, half, D_MODEL)])


def _ffn_moba_kernel(pt_ref, x_ref, g_ref, w1_ref, w3_ref, w2_ref, q_ref, kn_ref, vnew_ref,
                     ck_hbm, cv_hbm, o_ref, yb_ref, a_ref, buf, sem, s_ref, **moba_params):
    ffn = _ffn_steps(x_ref, g_ref, w1_ref, w3_ref, w2_ref, o_ref, a_ref)
    moba = _moba_sample_steps(pt_ref, q_ref, kn_ref, vnew_ref, ck_hbm, cv_hbm, yb_ref,
                              buf, sem, s_ref, **moba_params)
    n_chunk_steps = 2 * moba_params["n_pages"] // _PAGES_PER_CHUNK
    every = max(1, n_chunk_steps // len(ffn))
    for c in moba:
        if (c + 1) % every == 0 and ffn:
            ffn.pop(0)()
    for piece in ffn:
        piece()


def _ffn_moba(x, g, w1, w3, w2, page_table, q, k_new, v_new, cache_k, cache_v, *, layer, tm, n_q):
    m = x.shape[0]
    n_req, n_pages = page_table.shape
    assert m // tm == n_req
    depth, n_pool, page_size = cache_k.shape[:3]
    ck = jnp.transpose(cache_k, (0, 1, 3, 4, 2)).reshape(depth * n_pool, MOBA_DIM, page_size)
    cv = jnp.transpose(cache_v, (0, 1, 3, 4, 2)).reshape(depth * n_pool, MOBA_DIM, page_size)
    chunk_keys = _PAGES_PER_CHUNK * page_size
    past_len = n_pages * page_size
    q_rows = n_q * MOBA_HEADS
    assert (2 * n_pages // _PAGES_PER_CHUNK) % _RING_SLOTS == 0
    tile = pl.BlockSpec((tm, D_MODEL), lambda i, pt: (i, 0))
    rows = pl.BlockSpec((SAMPLE_ROWS, MOBA_DIM), lambda i, pt: (i, 0))
    hbm = pl.BlockSpec(memory_space=pl.ANY)
    scratch = [pltpu.VMEM((tm, D_FF), BF16),
               pltpu.VMEM((_RING_SLOTS, MOBA_DIM, chunk_keys), F32),
               pltpu.SemaphoreType.DMA((_RING_SLOTS,)),
               pltpu.VMEM((q_rows, past_len + LANES), F32)]
    blocks = (2 * 2 * _nbytes((tm, D_MODEL), F32) + 3 * _nbytes((D_MODEL, D_FF), BF16)
              + _nbytes((tm, D_FF), BF16)
              + _nbytes((_RING_SLOTS, MOBA_DIM, chunk_keys), F32)
              + _nbytes((q_rows, past_len + LANES), F32) + 8 * _nbytes((SAMPLE_ROWS, MOBA_DIM), F32))
    return pl.pallas_call(
        functools.partial(_ffn_moba_kernel, page_base=layer * n_pool, n_pages=n_pages,
                          page_size=page_size, n_q=n_q),
        grid_spec=pltpu.PrefetchScalarGridSpec(
            num_scalar_prefetch=1,
            grid=(n_req,),
            in_specs=[tile, _resident((1, D_MODEL)), _layer_resident(w1, layer),
                      _layer_resident(w3, layer), _layer_resident(w2, layer),
                      rows, rows, rows, hbm, hbm],
            out_specs=[tile, rows],
            scratch_shapes=scratch),
        out_shape=[jax.ShapeDtypeStruct((m, D_MODEL), F32),
                   jax.ShapeDtypeStruct((n_req * SAMPLE_ROWS, MOBA_DIM), F32)],
        compiler_params=pltpu.CompilerParams(
            dimension_semantics=("arbitrary",), vmem_limit_bytes=_vmem_limit(blocks)),
        name="ffn_moba",
    )(page_table.reshape(-1), x, g, w1, w3, w2, q, k_new, v_new, ck, cv)


def _sample_sgu_tables(sgu_w, sgu_b, n_b, t_new):
    t = jnp.arange(SAMPLE_ROWS)
    d = jnp.arange(t_new)
    src = t[None, :] - d[:, None]
    ok = (src >= 0) & (t[None, :] < t_new)
    w_td = sgu_w[:, t[None, :].clip(0, t_new - 1), src.clip(0, t_new - 1)]
    w_td = jnp.where(ok[None], w_td, 0.0)
    wd = jnp.repeat(jnp.transpose(w_td, (1, 2, 0)), SGU_GROUP_DIM, axis=2)
    bl = jnp.where((t < t_new)[:, None],
                   jnp.repeat(sgu_b[:, t.clip(0, t_new - 1)].T, SGU_GROUP_DIM, axis=1), 0.0)
    return jnp.tile(wd, (1, n_b, 1)), jnp.tile(bl, (n_b, 1))


def kernel(x_prompt, x_sample, mem_prompt, cache_k, cache_v, cache_mem_k, cache_mem_v, page_table,
           ffn1_norm, ffn1_w1, ffn1_w3, ffn1_w2, mix_norm, w_in, sgu_ln_g, sgu_ln_b, sgu_w, sgu_b,
           mem_norm, w_mem_kv, w_a_out, w_b_out, w_c_out, w_o, ffn2_norm, ffn2_w1, ffn2_w3, ffn2_w2,
           final_norm):
    depth = w_in.shape[0]
    bsz, t_len, _ = x_prompt.shape
    dec_b, t_new, _ = x_sample.shape
    n_pages = page_table.shape[1]
    page_size = cache_k.shape[2]
    past_len = n_pages * page_size
    mem_len = mem_prompt.shape[1]
    assert t_len % 512 == 0 and t_new <= SAMPLE_ROWS and past_len % MOBA_BLOCK == 0
    assert n_pages % _PAGES_PER_CHUNK == 0 and (_PAGES_PER_CHUNK * page_size) % MOBA_BLOCK == 0

    row = lambda a: a.reshape(1, -1)
    fin = row(final_norm)
    f1 = tuple(w.astype(BF16) for w in (ffn1_w1, ffn1_w3, ffn1_w2))
    f2 = tuple(w.astype(BF16) for w in (ffn2_w1, ffn2_w3, ffn2_w2))
    win, wkv = w_in.astype(BF16), w_mem_kv.astype(BF16)
    outs_w = tuple(w.astype(BF16) for w in (w_a_out, w_b_out, w_c_out, w_o))

    tm_p = 512
    cos_p, slo_p, shi_p = _rope_tables(jnp.arange(t_len, dtype=jnp.int32))
    hp = x_prompt.reshape(bsz * t_len, D_MODEL)
    mem = mem_prompt.reshape(bsz * mem_len, D_MODEL)
    kv_t, mkp, mvp = (), [], []

    m_s = dec_b * SAMPLE_ROWS
    pos_s = past_len + jnp.arange(SAMPLE_ROWS, dtype=jnp.int32)
    cos_s, slo_s, shi_s = (jnp.tile(t, (dec_b, 1)) for t in _rope_tables(pos_s))
    hs = jnp.pad(x_sample, ((0, 0), (0, SAMPLE_ROWS - t_new), (0, 0))).reshape(m_s, D_MODEL)
    n_b = 8
    mem_k = cache_mem_k.reshape(depth, dec_b, mem_len * MEM_HEADS, MEM_HEAD_DIM)
    mem_v = cache_mem_v.reshape(depth, dec_b, mem_len * MEM_HEADS, MEM_HEAD_DIM)
    ks_, vs_, sv_ = [], [], []
    unpad = lambda a: a.reshape(dec_b, SAMPLE_ROWS, -1)[:, :t_new]

    for l in range(depth):
        wd, bl = _sample_sgu_tables(sgu_w[l], sgu_b[l], n_b, t_new)
        hs = _ffn(hs, row(ffn1_norm[l]), *f1, fin, layer=l, tm=m_s, final_norm=False)
        u_s, vn_s, q_s, k_s, v_s, cq_s, gate_s = _inproj(
            hs, row(mix_norm[l]), win, row(sgu_ln_g[l]), row(sgu_ln_b[l]), cos_s, slo_s, shi_s,
            layer=l, tm=m_s, t_major_kv=False, prev_kv_t=(), act_dtype=F32)
        ya_s, yc_s = _premix_sample(u_s, vn_s, cq_s, mem_k, mem_v, wd, bl, layer=l, n_b=n_b)

        mk, mv, mk_c, mv_c = _memkv(mem, row(mem_norm[l]), wkv, layer=l, tm=512)
        hp, yb_s = _ffn_moba(hp, row(ffn1_norm[l]), *f1, page_table, q_s, k_s, v_s, cache_k, cache_v,
                             layer=l, tm=tm_p, n_q=t_new)
        u, vn, q, k, v, cq, gate, *kv_t = _inproj(
            hp, row(mix_norm[l]), win, row(sgu_ln_g[l]), row(sgu_ln_b[l]), cos_p, slo_p, shi_p,
            layer=l, tm=tm_p, t_major_kv=True, prev_kv_t=kv_t, act_dtype=BF16)
        yb = _moba_prompt(q.reshape(bsz, t_len, MOBA_DIM), k.reshape(bsz, t_len, MOBA_DIM),
                          v.reshape(bsz, t_len, MOBA_DIM), out_dtype=BF16).reshape(bsz * t_len, MOBA_DIM)
        hp = _mix_prompt(u, vn, cq, gate, yb, hp, mk.reshape(bsz, mem_len, MEM_DIM),
                         mv.reshape(bsz, mem_len, MEM_DIM), sgu_w, sgu_b[l].T, *outs_w,
                         layer=l, tm=tm_p, t_len=t_len)
        hp = _ffn(hp, row(ffn2_norm[l]), *f2, fin, layer=l, tm=2 * tm_p, final_norm=(l == depth - 1))
        mkp.append(mk_c.reshape(bsz, mem_len, MEM_HEADS, MEM_HEAD_DIM))
        mvp.append(mv_c.reshape(bsz, mem_len, MEM_HEADS, MEM_HEAD_DIM))

        hs = _merge_sample(hs, ya_s, yb_s, yc_s, gate_s, *outs_w, layer=l)
        hs = _ffn(hs, row(ffn2_norm[l]), *f2, fin, layer=l, tm=m_s, final_norm=(l == depth - 1))
        ks_.append(unpad(k_s).reshape(dec_b, t_new, MOBA_HEADS, MOBA_HEAD_DIM))
        vs_.append(unpad(v_s).reshape(dec_b, t_new, MOBA_HEADS, MOBA_HEAD_DIM))
        sv_.append(unpad(vn_s))
    y_prompt = hp.reshape(bsz, t_len, D_MODEL)
    y_sample = unpad(hs)

    def token_minor_to_rows(stack):
        return jnp.transpose(stack.reshape(depth, bsz, MOBA_HEADS, MOBA_HEAD_DIM, t_len), (0, 1, 4, 2, 3))

    return (y_prompt, y_sample, token_minor_to_rows(kv_t[0]), token_minor_to_rows(kv_t[1]),
            jnp.stack(mkp), jnp.stack(mvp), jnp.stack(ks_), jnp.stack(vs_), jnp.stack(sv_))
```
